```python
import math
import jax, jax.numpy as jnp
from jax import lax
import numpy as np

D_MODEL = 1024
BATCH = 2
SEQ = 8192
DEPTH = 4
DEC_BATCH = 32
DEC_SEQ = 8
PAST_LEN = 8192
PAGE_SIZE = 128

N_HEADS = 8
HEAD_DIM = 64
ATTN_WIDTH = N_HEADS * HEAD_DIM
IDX_HEADS = 8
IDX_DIM = 64
TOP_K_MAX = 256
Q_BLOCK = 128
SSM_WIDTH = 512
SSM_GROUP = 16
SSM_GROUPS = SSM_WIDTH // SSM_GROUP
SSM_STATE = 64
N_MEM = 256
X_HEADS = 4
X_HEAD_DIM = 128
X_WIDTH = X_HEADS * X_HEAD_DIM
D_FF = 4 * D_MODEL
N_BUCKETS = 32
MAX_DISTANCE = 128
EPS = 1e-6
IN_SIZES = (ATTN_WIDTH, ATTN_WIDTH, ATTN_WIDTH, IDX_HEADS * IDX_DIM, IDX_DIM, IDX_HEADS, SSM_WIDTH, 2 * D_MODEL)
IN_WIDTH = ATTN_WIDTH * 3 + IDX_HEADS * IDX_DIM + IDX_DIM + IDX_HEADS + SSM_WIDTH + 2 * D_MODEL

kernel_name = "dsa_s5_gated_hybrid_decode_step"


def rmsnorm(x, g):
    xf = x.astype(jnp.float32)
    y = xf * lax.rsqrt(jnp.mean(xf * xf, axis=-1, keepdims=True) + EPS)
    return (y * g.astype(jnp.float32)).astype(x.dtype)


def t5_bucket(dist):
    max_exact = N_BUCKETS // 2
    d = jnp.maximum(dist, 0)
    df = jnp.maximum(d, 1).astype(jnp.float32)
    large = max_exact + (jnp.log(df / max_exact) / math.log(MAX_DISTANCE / max_exact)
                         * (N_BUCKETS - max_exact)).astype(jnp.int32)
    large = jnp.minimum(large, N_BUCKETS - 1)
    return jnp.where(d < max_exact, d, large)


def mixer_inputs(x, g, w_in):
    n, t = x.shape[0], x.shape[1]
    h = rmsnorm(x, g)
    z = h @ w_in
    q, k, v, iq, ik, iw, u, gates = jnp.split(z, [int(c) for c in np.cumsum(IN_SIZES)[:-1]], axis=-1)
    return (q.reshape(n, t, N_HEADS, HEAD_DIM), k.reshape(n, t, N_HEADS, HEAD_DIM),
            v.reshape(n, t, N_HEADS, HEAD_DIM), iq.reshape(n, t, IDX_HEADS, IDX_DIM),
            ik, iw, u, gates)


def indexer_scores(iq, iw, ik):
    dots = jnp.einsum('nthd,nld->nthl', iq, ik).astype(jnp.float32) * IDX_DIM ** -0.5
    w = iw.astype(jnp.float32) * IDX_HEADS ** -0.5
    return jnp.einsum('nth,nthl->ntl', w, jax.nn.relu(dots))


def sparse_attend(q, k_sel, v_sel, dist, rel_bias):
    s = jnp.einsum('nthd,ntkhd->nhtk', q, k_sel).astype(jnp.float32) * HEAD_DIM ** -0.5
    bias = rel_bias[t5_bucket(dist)].astype(jnp.float32)
    s = s + jnp.transpose(bias, (0, 3, 1, 2))
    s = jnp.where((dist >= 0)[:, None], s, jnp.finfo(jnp.float32).min)
    p = jax.nn.softmax(s, axis=-1).astype(v_sel.dtype)
    o = jnp.einsum('nhtk,ntkhd->nthd', p, v_sel)
    return o.reshape(o.shape[0], o.shape[1], ATTN_WIDTH)


def dsa_prompt(q, k, v, iq, iw, ik, rel_bias):
    n, seq = q.shape[0], q.shape[1]
    top_k = min(TOP_K_MAX, seq // 4)
    key_pos = jnp.arange(seq)
    gather = jax.vmap(lambda a, i: a[i])

    def block(i):
        t0 = i * Q_BLOCK
        qb = lax.dynamic_slice_in_dim(q, t0, Q_BLOCK, axis=1)
        iqb = lax.dynamic_slice_in_dim(iq, t0, Q_BLOCK, axis=1)
        iwb = lax.dynamic_slice_in_dim(iw, t0, Q_BLOCK, axis=1)
        qpos = t0 + jnp.arange(Q_BLOCK)
        sc = indexer_scores(iqb, iwb, ik)
        sc = jnp.where((key_pos[None, :] <= qpos[:, None])[None], sc, jnp.finfo(jnp.float32).min)
        _, idx = lax.top_k(sc, top_k)
        dist = qpos[None, :, None] - idx
        return sparse_attend(qb, gather(k, idx), gather(v, idx), dist, rel_bias)

    out = lax.map(block, jnp.arange(seq // Q_BLOCK))
    return jnp.transpose(out, (1, 0, 2, 3)).reshape(n, seq, ATTN_WIDTH)


def dsa_sample(q, k_new, v_new, iq, iw, ik_new, pool_k, pool_v, pool_ik, page_table, rel_bias):
    n, t = q.shape[0], q.shape[1]
    n_pages = page_table.shape[1]
    past = n_pages * PAGE_SIZE
    total = past + t
    top_k = min(TOP_K_MAX, total // 4)
    ik_past = pool_ik[page_table].reshape(n, past, IDX_DIM)
    ik_all = jnp.concatenate([ik_past, ik_new.astype(ik_past.dtype)], axis=1)
    qpos = past + jnp.arange(t)
    sc = indexer_scores(iq, iw, ik_all)
    sc = jnp.where((jnp.arange(total)[None, :] <= qpos[:, None])[None], sc, jnp.finfo(jnp.float32).min)
    _, idx = lax.top_k(sc, top_k)
    pidx = jnp.minimum(idx, past - 1)
    phys = page_table[jnp.arange(n)[:, None, None], pidx // PAGE_SIZE]
    off = pidx % PAGE_SIZE
    nidx = jnp.clip(idx - past, 0, t - 1)
    gather = jax.vmap(lambda a, i: a[i])
    is_past = (idx < past)[..., None, None]
    k_sel = jnp.where(is_past, pool_k[phys, off], gather(k_new, nidx).astype(pool_k.dtype))
    v_sel = jnp.where(is_past, pool_v[phys, off], gather(v_new, nidx).astype(pool_v.dtype))
    dist = qpos[None, :, None] - idx
    return sparse_attend(q, k_sel.astype(q.dtype), v_sel.astype(q.dtype), dist, rel_bias)


def ssm_branch(u, x0_re, x0_im, a_re, a_im, b_re, b_im, c_re, c_im, d_skip, log_dt, w_glu):
    f32 = jnp.float32
    n, t = u.shape[0], u.shape[1]
    a_re, a_im = a_re.astype(f32), a_im.astype(f32)
    b_re, b_im = b_re.astype(f32), b_im.astype(f32)
    dt = jnp.exp(log_dt.astype(f32))[:, None]
    mag = jnp.exp(a_re * dt)
    lam_re, lam_im = mag * jnp.cos(a_im * dt), mag * jnp.sin(a_im * dt)
    den = a_re * a_re + a_im * a_im
    nr, ni = lam_re - 1.0, lam_im
    f_re = (nr * a_re + ni * a_im) / den
    f_im = (ni * a_re - nr * a_im) / den
    bb_re = f_re[..., None] * b_re - f_im[..., None] * b_im
    bb_im = f_re[..., None] * b_im + f_im[..., None] * b_re
    uf = u.astype(f32).reshape(n, t, SSM_GROUPS, SSM_GROUP)
    bu_re = jnp.einsum('ntgc,gpc->ntgp', uf, bb_re)
    bu_im = jnp.einsum('ntgc,gpc->ntgp', uf, bb_im)
    al_re = jnp.broadcast_to(lam_re, bu_re.shape)
    al_im = jnp.broadcast_to(lam_im, bu_im.shape)

    def combine(e1, e2):
        a1r, a1i, b1r, b1i = e1
        a2r, a2i, b2r, b2i = e2
        return (a1r * a2r - a1i * a2i, a1r * a2i + a1i * a2r,
                a2r * b1r - a2i * b1i + b2r, a2r * b1i + a2i * b1r + b2i)

    ar, ai, xr, xi = lax.associative_scan(combine, (al_re, al_im, bu_re, bu_im), axis=1)
    s0r, s0i = x0_re.astype(f32)[:, None], x0_im.astype(f32)[:, None]
    xr = xr + ar * s0r - ai * s0i
    xi = xi + ar * s0i + ai * s0r
    y = (jnp.einsum('ntgp,gcp->ntgc', xr, c_re.astype(f32))
         - jnp.einsum('ntgp,gcp->ntgc', xi, c_im.astype(f32))).reshape(n, t, SSM_WIDTH)
    y = (y + d_skip.astype(f32) * uf.reshape(n, t, SSM_WIDTH)).astype(u.dtype)
    y = jax.nn.gelu(y)
    y = y * jax.nn.sigmoid(y @ w_glu)
    return y, xr[:, -1].astype(x0_re.dtype), xi[:, -1].astype(x0_im.dtype)


def merge_branches(attn, ssm, gates, w_up_attn, w_up_ssm, w_out):
    g_attn, g_ssm = jnp.split(jax.nn.sigmoid(gates), 2, axis=-1)
    return (g_attn * (attn @ w_up_attn) + g_ssm * (ssm @ w_up_ssm)) @ w_out


def memory_kv(mem, g, w_k, w_v):
    n, m = mem.shape[0], mem.shape[1]
    hm = rmsnorm(mem, g)
    return (hm @ w_k).reshape(n, m, X_HEADS, X_HEAD_DIM), (hm @ w_v).reshape(n, m, X_HEADS, X_HEAD_DIM)


def cross_attend(x, mk, mv, g, w_q, w_o):
    n, t = x.shape[0], x.shape[1]
    q = (rmsnorm(x, g) @ w_q).reshape(n, t, X_HEADS, X_HEAD_DIM)
    s = jnp.einsum('nthd,nmhd->nhtm', q, mk.astype(q.dtype)).astype(jnp.float32) * X_HEAD_DIM ** -0.5
    p = jax.nn.softmax(s, axis=-1).astype(q.dtype)
    o = jnp.einsum('nhtm,nmhd->nthd', p, mv.astype(q.dtype)).reshape(n, t, X_WIDTH)
    return o @ w_o


def mlp(x, g, w_up, w_down):
    return jnp.square(jax.nn.relu(rmsnorm(x, g) @ w_up)) @ w_down


def setup_inputs(seed: int = 0) -> dict:
    key = jax.random.key(seed)
    ks = iter(jax.random.split(key, 64))
    f32 = jnp.float32

    def nrm(shape, scale):
        return jax.random.normal(next(ks), shape, f32) * scale

    def gain(shape):
        return 1.0 + 0.02 * jax.random.normal(next(ks), shape, f32)

    n_pages = PAST_LEN // PAGE_SIZE
    n_used = DEC_BATCH * n_pages
    n_pool = n_used + max(1, n_used // 4)
    page_table = jax.random.permutation(next(ks), n_pool)[:n_used].astype(jnp.int32).reshape(DEC_BATCH, n_pages)
    a_re = -0.5 + 0.01 * jax.random.normal(next(ks), (DEPTH, SSM_GROUPS, SSM_STATE), f32)
    a_im = math.pi * jnp.arange(SSM_STATE, dtype=f32) + 0.01 * jax.random.normal(next(ks), (DEPTH, SSM_GROUPS, SSM_STATE), f32)
    log_dt = jax.random.uniform(next(ks), (DEPTH, SSM_GROUPS), f32, math.log(0.001), math.log(0.1))
    return {
        "x_prompt": nrm((BATCH, SEQ, D_MODEL), 1.0),
        "x_sample": nrm((DEC_BATCH, DEC_SEQ, D_MODEL), 1.0),
        "mem_prompt": nrm((BATCH, N_MEM, D_MODEL), 1.0),
        "cache_k": nrm((DEPTH, n_pool, PAGE_SIZE, N_HEADS, HEAD_DIM), 1.0),
        "cache_v": nrm((DEPTH, n_pool, PAGE_SIZE, N_HEADS, HEAD_DIM), 1.0),
        "cache_idx_k": nrm((DEPTH, n_pool, PAGE_SIZE, IDX_DIM), 1.0),
        "state_ssm_re": nrm((DEPTH, DEC_BATCH, SSM_GROUPS, SSM_STATE), 0.3),
        "state_ssm_im": nrm((DEPTH, DEC_BATCH, SSM_GROUPS, SSM_STATE), 0.3),
        "cache_mem_k": nrm((DEPTH, DEC_BATCH, N_MEM, X_HEADS, X_HEAD_DIM), 1.0),
        "cache_mem_v": nrm((DEPTH, DEC_BATCH, N_MEM, X_HEADS, X_HEAD_DIM), 1.0),
        "page_table": page_table,
        "rel_bias": nrm((N_BUCKETS, N_HEADS), 0.5),
        "norm_mix": gain((DEPTH, D_MODEL)),
        "w_in": nrm((DEPTH, D_MODEL, IN_WIDTH), D_MODEL ** -0.5),
        "ssm_a_re": a_re,
        "ssm_a_im": a_im,
        "ssm_b_re": nrm((DEPTH, SSM_GROUPS, SSM_STATE, SSM_GROUP), (2 * SSM_GROUP) ** -0.5),
        "ssm_b_im": nrm((DEPTH, SSM_GROUPS, SSM_STATE, SSM_GROUP), (2 * SSM_GROUP) ** -0.5),
        "ssm_c_re": nrm((DEPTH, SSM_GROUPS, SSM_GROUP, SSM_STATE), SSM_STATE ** -0.5),
        "ssm_c_im": nrm((DEPTH, SSM_GROUPS, SSM_GROUP, SSM_STATE), SSM_STATE ** -0.5),
        "ssm_d": nrm((DEPTH, SSM_WIDTH), 1.0),
        "ssm_log_dt": log_dt,
        "w_glu": nrm((DEPTH, SSM_WIDTH, SSM_WIDTH), SSM_WIDTH ** -0.5),
        "w_up_attn": nrm((DEPTH, ATTN_WIDTH, D_MODEL), ATTN_WIDTH ** -0.5),
        "w_up_ssm": nrm((DEPTH, SSM_WIDTH, D_MODEL), SSM_WIDTH ** -0.5),
        "w_out": nrm((DEPTH, D_MODEL, D_MODEL), D_MODEL ** -0.5),
        "norm_cross": gain((DEPTH, D_MODEL)),
        "norm_mem": gain((DEPTH, D_MODEL)),
        "w_xq": nrm((DEPTH, D_MODEL, X_WIDTH), D_MODEL ** -0.5),
        "w_xk": nrm((DEPTH, D_MODEL, X_WIDTH), D_MODEL ** -0.5),
        "w_xv": nrm((DEPTH, D_MODEL, X_WIDTH), D_MODEL ** -0.5),
        "w_xo": nrm((DEPTH, X_WIDTH, D_MODEL), X_WIDTH ** -0.5),
        "norm_mlp": gain((DEPTH, D_MODEL)),
        "w_mlp_up": nrm((DEPTH, D_MODEL, D_FF), D_MODEL ** -0.5),
        "w_mlp_down": nrm((DEPTH, D_FF, D_MODEL), D_FF ** -0.5),
        "norm_final": gain((D_MODEL,)),
    }


def reference(x_prompt, x_sample, mem_prompt, cache_k, cache_v, cache_idx_k, state_ssm_re, state_ssm_im,
              cache_mem_k, cache_mem_v, page_table, rel_bias, norm_mix, w_in, ssm_a_re, ssm_a_im,
              ssm_b_re, ssm_b_im, ssm_c_re, ssm_c_im, ssm_d, ssm_log_dt, w_glu, w_up_attn, w_up_ssm, w_out,
              norm_cross, norm_mem, w_xq, w_xk, w_xv, w_xo, norm_mlp, w_mlp_up, w_mlp_down, norm_final):
    nb = x_prompt.shape[0]
    yp, ys = x_prompt, x_sample
    kp, vp, ikp, srp, sip, mkp, mvp = [], [], [], [], [], [], []
    ks_, vs_, iks, srs, sis = [], [], [], [], []
    zero_state = jnp.zeros((nb, SSM_GROUPS, SSM_STATE), x_prompt.dtype)
    for l in range(DEPTH):
        ssm_params = (ssm_a_re[l], ssm_a_im[l], ssm_b_re[l], ssm_b_im[l], ssm_c_re[l], ssm_c_im[l],
                      ssm_d[l], ssm_log_dt[l], w_glu[l])
        q, k, v, iq, ik, iw, u, gates = mixer_inputs(yp, norm_mix[l], w_in[l])
        attn = dsa_prompt(q, k, v, iq, iw, ik, rel_bias)
        ssm, s_re, s_im = ssm_branch(u, zero_state, zero_state, *ssm_params)
        yp = yp + merge_branches(attn, ssm, gates, w_up_attn[l], w_up_ssm[l], w_out[l])
        mk, mv = memory_kv(mem_prompt, norm_mem[l], w_xk[l], w_xv[l])
        yp = yp + cross_attend(yp, mk, mv, norm_cross[l], w_xq[l], w_xo[l])
        yp = yp + mlp(yp, norm_mlp[l], w_mlp_up[l], w_mlp_down[l])
        kp.append(k); vp.append(v); ikp.append(ik); srp.append(s_re); sip.append(s_im)
        mkp.append(mk); mvp.append(mv)
        q, k, v, iq, ik, iw, u, gates = mixer_inputs(ys, norm_mix[l], w_in[l])
        attn = dsa_sample(q, k, v, iq, iw, ik, cache_k[l], cache_v[l], cache_idx_k[l], page_table, rel_bias)
        ssm, s_re, s_im = ssm_branch(u, state_ssm_re[l], state_ssm_im[l], *ssm_params)
        ys = ys + merge_branches(attn, ssm, gates, w_up_attn[l], w_up_ssm[l], w_out[l])
        ys = ys + cross_attend(ys, cache_mem_k[l], cache_mem_v[l], norm_cross[l], w_xq[l], w_xo[l])
        ys = ys + mlp(ys, norm_mlp[l], w_mlp_up[l], w_mlp_down[l])
        ks_.append(k); vs_.append(v); iks.append(ik); srs.append(s_re); sis.append(s_im)
    y_prompt = rmsnorm(yp, norm_final)
    y_sample = rmsnorm(ys, norm_final)
    return (y_prompt, y_sample,
            jnp.stack(kp), jnp.stack(vp), jnp.stack(ikp), jnp.stack(srp), jnp.stack(sip),
            jnp.stack(mkp), jnp.stack(mvp),
            jnp.stack(ks_), jnp.stack(vs_), jnp.stack(iks), jnp.stack(srs), jnp.stack(sis))
```

```python
import functools
import math

import jax
import jax.numpy as jnp
import numpy as np
from jax import lax
from jax.experimental import pallas as pl
from jax.experimental.pallas import tpu as pltpu

F32 = jnp.float32
BF16 = jnp.bfloat16
I32 = jnp.int32

EPS = 1e-6
N_HEADS = 8
HEAD_DIM = 64
ATTN_WIDTH = N_HEADS * HEAD_DIM
IDX_HEADS = 8
IDX_DIM = 64
TOP_K_MAX = 256
SSM_GROUP = 16
SSM_STATE = 64
X_HEADS = 4
X_HEAD_DIM = 128
N_BUCKETS = 32
MAX_DISTANCE = 128
PAGE_SIZE = 128

LANES = 128
SUBLANES = 8
VMEM_LIMIT_BYTES = 56 * 1024 * 1024
INT_MIN = -(2 ** 31)
INT_MAX = 2 ** 31 - 1
F32_MIN = float(np.finfo(np.float32).min)
F32_MAX = float(np.finfo(np.float32).max)
M_INIT = -1e30

NT_DIMS = (((1,), (1,)), ((), ()))


def _params(*semantics):
    return pltpu.CompilerParams(dimension_semantics=semantics, vmem_limit_bytes=VMEM_LIMIT_BYTES)


def _rmsnorm(x, g):
    ms = jnp.mean(x * x, axis=-1, keepdims=True)
    return x * lax.rsqrt(ms + EPS) * g


def _dot(a, b):
    return jnp.dot(a, b, preferred_element_type=F32)


def _dot_nt(a, b):
    return lax.dot_general(a, b, NT_DIMS, preferred_element_type=F32)


def _monotone_key(x):
    b = lax.bitcast_convert_type(x, I32)
    key = jnp.where(b < 0, b ^ INT_MAX, b)
    return jnp.where(x == 0.0, 0, key)


def _row_tile(rows, want):
    t = min(rows, want)
    assert rows % t == 0, (rows, t)
    return t


def _t5_bucket(dist):
    max_exact = N_BUCKETS // 2
    d = jnp.maximum(dist, 0)
    df = jnp.maximum(d, 1).astype(F32)
    large = max_exact + (jnp.log(df / max_exact) / math.log(MAX_DISTANCE / max_exact)
                         * (N_BUCKETS - max_exact)).astype(I32)
    large = jnp.minimum(large, N_BUCKETS - 1)
    return jnp.where(d < max_exact, d, large)


def _bias_lookup(rel_ref, bucket, h):
    out = jnp.zeros(bucket.shape, F32)
    for b in range(N_BUCKETS):
        out = jnp.where(bucket == b, rel_ref[b, h], out)
    return out - rel_ref[N_BUCKETS - 1, h]


def _bias_prompt_kernel(rel_ref, o_ref, *, tq):
    r = lax.broadcasted_iota(I32, (2 * tq, tq), 0)
    j = lax.broadcasted_iota(I32, (2 * tq, tq), 1)
    bucket = _t5_bucket(j - r + tq)
    for h in range(N_HEADS):
        o_ref[h] = _bias_lookup(rel_ref, bucket, h)


def _bias_sample_kernel(rel_ref, last_ref, new_ref, *, t_new):
    row = lax.broadcasted_iota(I32, (N_HEADS * t_new, PAGE_SIZE), 0)
    lane = lax.broadcasted_iota(I32, (N_HEADS * t_new, PAGE_SIZE), 1)
    last = jnp.zeros(row.shape, F32)
    new = jnp.zeros(row.shape, F32)
    for h in range(N_HEADS):
        t = row - h * t_new
        in_head = (t >= 0) & (t < t_new)
        last = jnp.where(in_head, _bias_lookup(rel_ref, _t5_bucket(PAGE_SIZE + t - lane), h), last)
        new = jnp.where(in_head, _bias_lookup(rel_ref, _t5_bucket(t - lane), h), new)
    last_ref[...] = last
    new_ref[...] = new


def _bias_tables(rel_bias, tq, t_new):
    smem = pl.BlockSpec(memory_space=pltpu.SMEM)
    prompt = pl.pallas_call(
        functools.partial(_bias_prompt_kernel, tq=tq),
        out_shape=jax.ShapeDtypeStruct((N_HEADS, 2 * tq, tq), F32),
        in_specs=[smem], name="bias_prompt")(rel_bias)
    last, new = pl.pallas_call(
        functools.partial(_bias_sample_kernel, t_new=t_new),
        out_shape=[jax.ShapeDtypeStruct((N_HEADS * t_new, PAGE_SIZE), F32)] * 2,
        in_specs=[smem], name="bias_sample")(rel_bias)
    return prompt, last, new


def _proj_prompt_kernel(x_ref, g_ref, wqT_ref, wiqT_ref, wiwT_ref, wk_ref, wv_ref, wvT_ref, wik_ref, wu_ref, wg_ref,
                        qT_ref, iqT_ref, iwT_ref, k_ref, v_ref, ik_ref, kb_ref, vT_ref, ikb_ref, u_ref, gate_ref):
    h = _rmsnorm(x_ref[0], g_ref[...]).astype(BF16)
    qT_ref[0] = _dot_nt(wqT_ref[...], h).astype(BF16)
    iqT_ref[0] = _dot_nt(wiqT_ref[...], h).astype(BF16)
    iwT_ref[0] = _dot_nt(wiwT_ref[...], h) * (IDX_HEADS ** -0.5)
    k = _dot(h, wk_ref[...])
    k_ref[0] = k
    kb_ref[0] = k.astype(BF16)
    v_ref[0] = _dot(h, wv_ref[...])
    vT_ref[0] = _dot_nt(wvT_ref[...], h).astype(BF16)
    ik = _dot(h, wik_ref[...])
    ik_ref[0] = ik
    ikb_ref[0] = ik.astype(BF16)
    u_ref[0] = _dot(h, wu_ref[...])
    gate_ref[0] = jax.nn.sigmoid(_dot(h, wg_ref[...]))


def _proj_rows_kernel(x_ref, g_ref, wq_ref, wiq_ref, wiw_ref, wk_ref, wv_ref, wik_ref, wu_ref, wg_ref,
                      q_ref, iq_ref, iw_ref, k_ref, v_ref, ik_ref, u_ref, gate_ref):
    h = _rmsnorm(x_ref[...], g_ref[...]).astype(BF16)
    q_ref[...] = _dot(h, wq_ref[...])
    iq_ref[...] = _dot(h, wiq_ref[...])
    iw_ref[...] = _dot(h, wiw_ref[...]) * (IDX_HEADS ** -0.5)
    k_ref[...] = _dot(h, wk_ref[...])
    v_ref[...] = _dot(h, wv_ref[...])
    ik_ref[...] = _dot(h, wik_ref[...])
    u_ref[...] = _dot(h, wu_ref[...])
    gate_ref[...] = jax.nn.sigmoid(_dot(h, wg_ref[...]))


def _split_w_in(w_in):
    sizes = (ATTN_WIDTH, ATTN_WIDTH, ATTN_WIDTH, IDX_HEADS * IDX_DIM, IDX_DIM, IDX_HEADS)
    offs = np.cumsum((0,) + sizes)
    d_model = w_in.shape[0]
    ssm_width = (w_in.shape[1] - offs[-1] - 2 * d_model)
    wq, wk, wv, wiq, wik, wiw = (w_in[:, offs[i]:offs[i + 1]] for i in range(6))
    wu = w_in[:, offs[-1]:offs[-1] + ssm_width]
    wg = w_in[:, offs[-1] + ssm_width:]
    wq = wq * (HEAD_DIM ** -0.5)
    wiq = wiq * (IDX_DIM ** -0.5)
    return tuple(w.astype(BF16) for w in (wq, wk, wv, wiq, wik, wiw, wu, wg))


def _full(shape):
    return pl.BlockSpec(shape, lambda *_: (0,) * len(shape))


def _proj_prompt(x, g, ws, tm):
    n, t, d = x.shape
    wq, wk, wv, wiq, wik, wiw, wu, wg = ws
    sw, gw = wu.shape[1], wg.shape[1]
    weights = (wq.T, wiq.T, wiw.T, wk, wv, wv.T, wik, wu, wg)
    rows = lambda w: pl.BlockSpec((1, tm, w), lambda b, i: (b, i, 0))
    cols = lambda w: pl.BlockSpec((1, w, tm), lambda b, i: (b, 0, i))
    out_shape = [
        jax.ShapeDtypeStruct((n, ATTN_WIDTH, t), BF16),
        jax.ShapeDtypeStruct((n, IDX_HEADS * IDX_DIM, t), BF16),
        jax.ShapeDtypeStruct((n, IDX_HEADS, t), F32),
        jax.ShapeDtypeStruct((n, t, ATTN_WIDTH), F32),
        jax.ShapeDtypeStruct((n, t, ATTN_WIDTH), F32),
        jax.ShapeDtypeStruct((n, t, IDX_DIM), F32),
        jax.ShapeDtypeStruct((n, t, ATTN_WIDTH), BF16),
        jax.ShapeDtypeStruct((n, ATTN_WIDTH, t), BF16),
        jax.ShapeDtypeStruct((n, t, IDX_DIM), BF16),
        jax.ShapeDtypeStruct((n, t, sw), F32),
        jax.ShapeDtypeStruct((n, t, gw), F32),
    ]
    out_specs = [cols(ATTN_WIDTH), cols(IDX_HEADS * IDX_DIM), cols(IDX_HEADS), rows(ATTN_WIDTH), rows(ATTN_WIDTH),
                 rows(IDX_DIM), rows(ATTN_WIDTH), cols(ATTN_WIDTH), rows(IDX_DIM), rows(sw), rows(gw)]
    return pl.pallas_call(
        _proj_prompt_kernel, grid=(n, t // tm), out_shape=out_shape,
        in_specs=[rows(d), _full((1, d))] + [_full(w.shape) for w in weights],
        out_specs=out_specs, compiler_params=_params("parallel", "parallel"), name="proj_prompt",
    )(x, g.reshape(1, d), *weights)


def _proj_rows(x, g, ws):
    r, d = x.shape
    wq, wk, wv, wiq, wik, wiw, wu, wg = ws
    weights = (wq, wiq, wiw, wk, wv, wik, wu, wg)
    out_shape = [jax.ShapeDtypeStruct((r, w.shape[1]), F32) for w in weights]
    return pl.pallas_call(
        _proj_rows_kernel, out_shape=out_shape, compiler_params=_params(), name="proj_rows",
    )(x, g.reshape(1, d), *weights)


def _dsa_prompt_kernel(qT_ref, iqT_ref, iwT_ref, ikb_ref, kb_ref, vT_ref, bias_ref, o_ref,
                       s_ref, q2_ref, acc_ref, m_ref, l_ref, j_ref, *, tq, top_k, idx_bits):
    i = pl.program_id(1)
    nk = i + 1
    iw = iwT_ref[0]
    row = lax.broadcasted_iota(I32, (tq, tq), 0)
    col = lax.broadcasted_iota(I32, (tq, tq), 1)

    def score_chunk(c, carry):
        r0 = pl.multiple_of(c * tq, tq)
        ikc = ikb_ref[0, pl.ds(r0, tq), :]
        sc = jnp.zeros((tq, tq), F32)
        for h in range(IDX_HEADS):
            d = _dot(ikc, iqT_ref[0, h * IDX_DIM:(h + 1) * IDX_DIM, :])
            sc = sc + iw[h:h + 1, :] * jnp.maximum(d, 0.0)
        key = _monotone_key(sc)
        key = jnp.where((c == i) & (row > col), INT_MIN, key)
        s_ref[pl.ds(r0, tq), :] = key
        return carry

    lax.fori_loop(0, nk, score_chunk, 0)

    def count(pred):
        def body(c, acc8):
            r0 = pl.multiple_of(c * tq, tq)
            ind = jnp.where(pred(s_ref[pl.ds(r0, tq), :], r0), 1, 0)
            return acc8 + ind.reshape(tq // SUBLANES, SUBLANES, tq).sum(axis=0)
        acc8 = lax.fori_loop(0, nk, body, jnp.zeros((SUBLANES, tq), I32))
        return acc8.sum(axis=0, keepdims=True)

    def bit_step(it, tu):
        cand_u = tu | lax.shift_left(jnp.int32(1), 31 - it)
        cand = cand_u ^ INT_MIN
        cnt = count(lambda blk, r0: blk >= cand)
        return jnp.where(cnt >= top_k, cand_u, tu)

    tu = lax.fori_loop(0, 32, bit_step, jnp.zeros((1, tq), I32))
    thr = jnp.maximum(tu ^ INT_MIN, INT_MIN + 1)
    cnt_gt = count(lambda blk, r0: blk > thr)
    cnt_ge = count(lambda blk, r0: blk >= thr)
    need = top_k - cnt_gt
    multi = cnt_ge > top_k

    j_ref[...] = jnp.full(j_ref.shape, INT_MAX, I32)

    @pl.when(jnp.max(multi.astype(I32)) > 0)
    def _():
        def j_step(it, jv):
            cand = jv | lax.shift_left(jnp.int32(1), idx_bits - 1 - it)
            cnt = count(lambda blk, r0: (blk == thr) & ((r0 + row) < cand))
            return jnp.where(cnt < need, cand, jv)
        jv = lax.fori_loop(0, idx_bits, j_step, jnp.zeros((1, tq), I32))
        j_ref[0:1, :] = jnp.where(multi, jv, INT_MAX)

    jsel = j_ref[0:1, :]

    upper = lax.broadcasted_iota(I32, (2 * HEAD_DIM, tq), 0) >= HEAD_DIM
    for h in range(N_HEADS):
        slab = qT_ref[0, (h // 2) * 2 * HEAD_DIM:(h // 2 + 1) * 2 * HEAD_DIM, :]
        q2_ref[h * 2 * HEAD_DIM:(h + 1) * 2 * HEAD_DIM, :] = jnp.where(upper == (h % 2 == 1), slab, jnp.zeros_like(slab))
    m_ref[...] = jnp.full(m_ref.shape, M_INIT, F32)
    l_ref[...] = jnp.zeros(l_ref.shape, F32)
    acc_ref[...] = jnp.zeros(acc_ref.shape, F32)

    def attend(c, bias_row0):
        r0 = pl.multiple_of(c * tq, tq)
        blk = s_ref[pl.ds(r0, tq), :]
        sel = (blk > thr) | ((blk == thr) & ((r0 + row) <= jsel))
        for h in range(N_HEADS):
            kslab = kb_ref[0, pl.ds(r0, tq), (h // 2) * 2 * HEAD_DIM:(h // 2 + 1) * 2 * HEAD_DIM]
            s = _dot(kslab, q2_ref[h * 2 * HEAD_DIM:(h + 1) * 2 * HEAD_DIM, :])
            if bias_row0 is not None:
                s = s + bias_ref[h, bias_row0:bias_row0 + tq, :]
            s = jnp.where(sel, s, F32_MIN)
            m_old = m_ref[h:h + 1, :]
            m_new = jnp.maximum(m_old, s.max(axis=0, keepdims=True))
            p = jnp.exp(s - m_new)
            alpha = jnp.exp(m_old - m_new)
            l_ref[h:h + 1, :] = alpha * l_ref[h:h + 1, :] + p.sum(axis=0, keepdims=True)
            pv = _dot(vT_ref[0, h * HEAD_DIM:(h + 1) * HEAD_DIM, pl.ds(r0, tq)], p.astype(BF16))
            acc_ref[h * HEAD_DIM:(h + 1) * HEAD_DIM, :] = alpha * acc_ref[h * HEAD_DIM:(h + 1) * HEAD_DIM, :] + pv
            m_ref[h:h + 1, :] = m_new

    def far_chunk(c, carry):
        attend(c, None)
        return carry

    lax.fori_loop(0, jnp.maximum(i - 1, 0), far_chunk, 0)

    @pl.when(i >= 1)
    def _():
        attend(i - 1, 0)

    attend(i, tq)

    inv_l = 1.0 / l_ref[...]
    for h in range(N_HEADS):
        acc_ref[h * HEAD_DIM:(h + 1) * HEAD_DIM, :] = acc_ref[h * HEAD_DIM:(h + 1) * HEAD_DIM, :] * inv_l[h:h + 1, :]
    o_ref[0] = acc_ref[...].T.astype(BF16)


def _dsa_prompt(qT, iqT, iwT, ikb, kb, vT, bias, tq):
    n, _, t = qT.shape
    assert t % tq == 0 and tq >= MAX_DISTANCE, (t, tq)
    top_k = min(TOP_K_MAX, t // 4)
    idx_bits = max(1, int(math.ceil(math.log2(t))))
    colblk = lambda w: pl.BlockSpec((1, w, tq), lambda b, i: (b, 0, i))
    whole = lambda a: pl.BlockSpec((1,) + a.shape[1:], lambda b, i: (b, 0, 0))
    kern = functools.partial(_dsa_prompt_kernel, tq=tq, top_k=top_k, idx_bits=idx_bits)
    return pl.pallas_call(
        kern, grid=(n, t // tq),
        out_shape=jax.ShapeDtypeStruct((n, t, ATTN_WIDTH), BF16),
        in_specs=[colblk(ATTN_WIDTH), colblk(IDX_HEADS * IDX_DIM), colblk(IDX_HEADS),
                  whole(ikb), whole(kb), whole(vT), _full(bias.shape)],
        out_specs=pl.BlockSpec((1, tq, ATTN_WIDTH), lambda b, i: (b, i, 0)),
        scratch_shapes=[pltpu.VMEM((t, tq), I32),
                        pltpu.VMEM((N_HEADS * 2 * HEAD_DIM, tq), BF16),
                        pltpu.VMEM((ATTN_WIDTH, tq), F32),
                        pltpu.VMEM((N_HEADS, tq), F32),
                        pltpu.VMEM((N_HEADS, tq), F32),
                        pltpu.VMEM((SUBLANES, tq), I32)],
        compiler_params=_params("parallel", "arbitrary"), name="dsa_prompt",
    )(qT, iqT, iwT, ikb, kb, vT, bias)


def _page_specs(width, pages_per_step):
    def spec(r):
        return pl.BlockSpec((1, PAGE_SIZE, width), lambda b, g, pt: (pt[b, g * pages_per_step + r], 0, 0))
    return [spec(r) for r in range(pages_per_step)]


def _head_sum(x, t_new):
    return x.reshape(IDX_HEADS, t_new, x.shape[-1]).sum(axis=0)


def _dsa_sample_score_kernel(pt_ref, *refs, pps, t_new):
    page_refs = refs[:pps]
    iq_ref, w_ref, iknew_ref, sp_ref, sn_ref = refs[pps:]
    g = pl.program_id(1)
    iq = iq_ref[0]
    w = w_ref[0]
    for r in range(pps):
        d = _dot_nt(iq, page_refs[r][0].astype(BF16))
        sc = _head_sum(w * jnp.maximum(d, 0.0), t_new)
        sp_ref[0, :, r * PAGE_SIZE:(r + 1) * PAGE_SIZE] = _monotone_key(sc)

    @pl.when(g == pl.num_programs(1) - 1)
    def _():
        d = _dot_nt(iq, iknew_ref[0].astype(BF16))
        sc = _head_sum(w * jnp.maximum(d, 0.0), t_new)
        t = lax.broadcasted_iota(I32, sc.shape, 0)
        j = lax.broadcasted_iota(I32, sc.shape, 1)
        sn_ref[0] = jnp.where(j > t, INT_MIN, _monotone_key(sc))


def _dsa_sample_thr_kernel(sp_ref, sn_ref, thr_ref, j_ref, *, top_k, idx_bits):
    rows, past = sp_ref.shape
    lane = lax.broadcasted_iota(I32, (rows, LANES), 1)

    def count(pred):
        tot = jnp.where(pred(sn_ref[...], past + lane), 1, 0)
        for c in range(past // LANES):
            tot = tot + jnp.where(pred(sp_ref[:, c * LANES:(c + 1) * LANES], c * LANES + lane), 1, 0)
        return jnp.broadcast_to(tot.sum(axis=-1, keepdims=True), (rows, LANES))

    def bit_step(it, tu):
        cand_u = tu | lax.shift_left(jnp.int32(1), 31 - it)
        cand = cand_u ^ INT_MIN
        cnt = count(lambda blk, idx: blk >= cand)
        return jnp.where(cnt >= top_k, cand_u, tu)

    tu = lax.fori_loop(0, 32, bit_step, jnp.zeros((rows, LANES), I32))
    thr = jnp.maximum(tu ^ INT_MIN, INT_MIN + 1)
    cnt_gt = count(lambda blk, idx: blk > thr)
    cnt_ge = count(lambda blk, idx: blk >= thr)
    need = top_k - cnt_gt
    multi = cnt_ge > top_k
    thr_ref[...] = thr
    j_ref[...] = jnp.full((rows, LANES), INT_MAX, I32)

    @pl.when(jnp.max(multi.astype(I32)) > 0)
    def _():
        def j_step(it, jv):
            cand = jv | lax.shift_left(jnp.int32(1), idx_bits - 1 - it)
            cnt = count(lambda blk, idx: (blk == thr) & (idx < cand))
            return jnp.where(cnt < need, cand, jv)
        jv = lax.fori_loop(0, idx_bits, j_step, jnp.zeros((rows, LANES), I32))
        j_ref[...] = jnp.where(multi, jv, INT_MAX)


def _dsa_sample_attn_kernel(pt_ref, *refs, pps, t_new):
    k_refs = refs[:pps]
    v_refs = refs[pps:2 * pps]
    (q_ref, sp_ref, sn_ref, thr_ref, j_ref, knew_ref, vnew_ref, blast_ref, bnew_ref,
     o_ref, acc_ref, m_ref, l_ref) = refs[2 * pps:]
    g = pl.program_id(1)
    last = g == pl.num_programs(1) - 1
    q = q_ref[0]
    thr = thr_ref[0]
    jsel = j_ref[0]
    lane = lax.broadcasted_iota(I32, (t_new, PAGE_SIZE), 1)

    @pl.when(g == 0)
    def _():
        m_ref[...] = jnp.full(m_ref.shape, M_INIT, F32)
        l_ref[...] = jnp.zeros(l_ref.shape, F32)
        acc_ref[...] = jnp.zeros(acc_ref.shape, F32)

    def update(keys, idx, kpage, vpage, bias):
        s = _dot_nt(q, kpage.astype(BF16))
        if bias is not None:
            s = s + bias
        sel = (keys > thr) | ((keys == thr) & (idx <= jsel))
        cap = jnp.where(sel, F32_MAX, F32_MIN)
        s = jnp.minimum(s, jnp.concatenate([cap] * N_HEADS, axis=0))
        m_old = m_ref[...]
        m_new = jnp.maximum(m_old, jnp.broadcast_to(s.max(axis=-1, keepdims=True), m_old.shape))
        p = jnp.exp(s - m_new)
        alpha = jnp.exp(m_old - m_new)
        l_ref[...] = alpha * l_ref[...] + jnp.broadcast_to(p.sum(axis=-1, keepdims=True), m_old.shape)
        acc_ref[...] = jnp.concatenate([alpha] * (ATTN_WIDTH // LANES), axis=1) * acc_ref[...] + _dot(
            p.astype(BF16), vpage.astype(BF16))
        m_ref[...] = m_new

    last_f = jnp.where(last, 1.0, 0.0)
    for r in range(pps):
        bias = blast_ref[...] * last_f if r == pps - 1 else None
        base = (g * pps + r) * PAGE_SIZE
        update(sp_ref[0, :, r * PAGE_SIZE:(r + 1) * PAGE_SIZE], base + lane, k_refs[r][0], v_refs[r][0], bias)

    @pl.when(last)
    def _():
        n_past = pl.num_programs(1) * pps * PAGE_SIZE
        update(sn_ref[0], n_past + lane, knew_ref[0], vnew_ref[0], bnew_ref[...])
        o_full = acc_ref[...] / jnp.concatenate([l_ref[...]] * (ATTN_WIDTH // LANES), axis=1)
        out_lane = lax.broadcasted_iota(I32, (t_new, ATTN_WIDTH), 1)
        out = jnp.zeros((t_new, ATTN_WIDTH), F32)
        for h in range(N_HEADS):
            in_head = (out_lane >= h * HEAD_DIM) & (out_lane < (h + 1) * HEAD_DIM)
            out = jnp.where(in_head, o_full[h * t_new:(h + 1) * t_new, :], out)
        o_ref[0] = out.astype(BF16)


def _dsa_sample(q, iq, iw, k_new, v_new, ik_new, pool_k, pool_v, pool_ik, page_table, bias_last, bias_new):
    n, t_new, _ = q.shape
    n_pages = page_table.shape[1]
    past = n_pages * PAGE_SIZE
    top_k = min(TOP_K_MAX, (past + t_new) // 4)
    pps = math.gcd(n_pages, 8)
    groups = n_pages // pps
    n_pool = pool_k.shape[0]
    pool_k = pool_k.reshape(n_pool, PAGE_SIZE, ATTN_WIDTH)
    pool_v = pool_v.reshape(n_pool, PAGE_SIZE, ATTN_WIDTH)
    ht = N_HEADS * t_new
    assert t_new <= PAGE_SIZE and IDX_HEADS == N_HEADS

    iq_hm = iq.reshape(n, t_new, IDX_HEADS, IDX_DIM).transpose(0, 2, 1, 3).reshape(n, ht, IDX_DIM).astype(BF16)
    w_hm = jnp.broadcast_to(iw.transpose(0, 2, 1).reshape(n, ht, 1), (n, ht, LANES))
    eye = jnp.eye(N_HEADS, dtype=F32)
    q_bd = (q.reshape(n, t_new, N_HEADS, HEAD_DIM).transpose(0, 2, 1, 3)[:, :, :, None, :]
            * eye[None, :, None, :, None]).reshape(n, ht, ATTN_WIDTH).astype(BF16)
    pad = lambda a: jnp.pad(a, ((0, 0), (0, PAGE_SIZE - t_new), (0, 0)))
    ik_pad, k_pad, v_pad = pad(ik_new), pad(k_new), pad(v_new)

    per_b = lambda a: pl.BlockSpec((1,) + a.shape[1:], lambda b, g, pt: (b,) + (0,) * (a.ndim - 1))
    const = lambda a: pl.BlockSpec(a.shape, lambda b, g, pt: (0,) * a.ndim)
    sp_spec = pl.BlockSpec((1, t_new, pps * PAGE_SIZE), lambda b, g, pt: (b, 0, g))

    sp, sn = pl.pallas_call(
        functools.partial(_dsa_sample_score_kernel, pps=pps, t_new=t_new),
        grid_spec=pltpu.PrefetchScalarGridSpec(
            num_scalar_prefetch=1, grid=(n, groups),
            in_specs=_page_specs(IDX_DIM, pps) + [per_b(iq_hm), per_b(w_hm), per_b(ik_pad)],
            out_specs=[sp_spec, pl.BlockSpec((1, t_new, PAGE_SIZE), lambda b, g, pt: (b, 0, 0))]),
        out_shape=[jax.ShapeDtypeStruct((n, t_new, past), I32), jax.ShapeDtypeStruct((n, t_new, PAGE_SIZE), I32)],
        compiler_params=_params("parallel", "arbitrary"), name="dsa_sample_score",
    )(page_table, *([pool_ik] * pps), iq_hm, w_hm, ik_pad)

    rows = n * t_new
    rt = _row_tile(rows, 32)
    idx_bits = int(math.ceil(math.log2(past + PAGE_SIZE)))
    thr, jsel = pl.pallas_call(
        functools.partial(_dsa_sample_thr_kernel, top_k=top_k, idx_bits=idx_bits),
        grid=(rows // rt,),
        in_specs=[pl.BlockSpec((rt, past), lambda i: (i, 0)), pl.BlockSpec((rt, PAGE_SIZE), lambda i: (i, 0))],
        out_specs=[pl.BlockSpec((rt, LANES), lambda i: (i, 0))] * 2,
        out_shape=[jax.ShapeDtypeStruct((rows, LANES), I32)] * 2,
        compiler_params=_params("parallel"), name="dsa_sample_thr",
    )(sp.reshape(rows, past), sn.reshape(rows, PAGE_SIZE))
    thr = thr.reshape(n, t_new, LANES)
    jsel = jsel.reshape(n, t_new, LANES)

    return pl.pallas_call(
        functools.partial(_dsa_sample_attn_kernel, pps=pps, t_new=t_new),
        grid_spec=pltpu.PrefetchScalarGridSpec(
            num_scalar_prefetch=1, grid=(n, groups),
            in_specs=(_page_specs(ATTN_WIDTH, pps) + _page_specs(ATTN_WIDTH, pps)
                      + [per_b(q_bd), sp_spec, per_b(sn), per_b(thr), per_b(jsel), per_b(k_pad), per_b(v_pad),
                         const(bias_last), const(bias_new)]),
            out_specs=pl.BlockSpec((1, t_new, ATTN_WIDTH), lambda b, g, pt: (b, 0, 0)),
            scratch_shapes=[pltpu.VMEM((ht, ATTN_WIDTH), F32), pltpu.VMEM((ht, LANES), F32),
                            pltpu.VMEM((ht, LANES), F32)]),
        out_shape=jax.ShapeDtypeStruct((n, t_new, ATTN_WIDTH), BF16),
        compiler_params=_params("parallel", "arbitrary"), name="dsa_sample_attn",
    )(page_table, *([pool_k] * pps), *([pool_v] * pps), q_bd, sp, sn, thr, jsel, k_pad, v_pad, bias_last, bias_new)


def _gelu_tanh(y):
    return 0.5 * y * (1.0 + jnp.tanh(math.sqrt(2.0 / math.pi) * (y + 0.044715 * (y * y * y))))


def _ssm_kernel(u_ref, x0r_ref, x0i_ref, lr_ref, li_ref, b_ref, c_ref, d_ref, wglu_ref,
                y_ref, sr_ref, si_ref, x_ref, st_ref, *, tile):
    j = pl.program_id(1)
    ns = lr_ref.shape[1]

    @pl.when(j == 0)
    def _():
        st_ref[0:1, :] = x0r_ref[0]
        st_ref[1:2, :] = x0i_ref[0]

    u = u_ref[0]
    x_ref[...] = _dot(u.astype(BF16), b_ref[...])
    lr = lr_ref[...]
    li = li_ref[...]

    def step(t, carry):
        sr, si = carry
        br = x_ref[pl.ds(t, 1), 0:ns]
        bi = x_ref[pl.ds(t, 1), ns:2 * ns]
        nr = lr * sr - li * si + br
        ni = lr * si + li * sr + bi
        x_ref[pl.ds(t, 1), 0:ns] = nr
        x_ref[pl.ds(t, 1), ns:2 * ns] = ni
        return nr, ni

    sr, si = lax.fori_loop(0, tile, step, (st_ref[0:1, :], st_ref[1:2, :]))
    st_ref[0:1, :] = sr
    st_ref[1:2, :] = si

    y = _dot(x_ref[...].astype(BF16), c_ref[...]) + d_ref[...] * u
    y = _gelu_tanh(y)
    y_ref[0] = (y * jax.nn.sigmoid(_dot(y.astype(BF16), wglu_ref[...]))).astype(BF16)

    @pl.when(j == pl.num_programs(1) - 1)
    def _():
        sr_ref[0] = sr
        si_ref[0] = si


def _ssm_params(a_re, a_im, b_re, b_im, c_re, c_im, log_dt):
    g = a_re.shape[0]
    dt = jnp.exp(log_dt)[:, None]
    mag = jnp.exp(a_re * dt)
    lam_re, lam_im = mag * jnp.cos(a_im * dt), mag * jnp.sin(a_im * dt)
    den = a_re * a_re + a_im * a_im
    nr, ni = lam_re - 1.0, lam_im
    f_re = (nr * a_re + ni * a_im) / den
    f_im = (ni * a_re - nr * a_im) / den
    bb_re = f_re[..., None] * b_re - f_im[..., None] * b_im
    bb_im = f_re[..., None] * b_im + f_im[..., None] * b_re
    eye = jnp.eye(g, dtype=F32)
    p, c = bb_re.shape[1], bb_re.shape[2]
    blk_in = lambda m: jnp.einsum('gpc,gh->gchp', m, eye).reshape(g * c, g * p)
    blk_out = lambda m: jnp.einsum('gcp,gh->gphc', m, eye).reshape(g * p, g * c)
    b_blk = jnp.concatenate([blk_in(bb_re), blk_in(bb_im)], axis=1).astype(BF16)
    c_blk = jnp.concatenate([blk_out(c_re), -blk_out(c_im)], axis=0).astype(BF16)
    return lam_re.reshape(1, g * p), lam_im.reshape(1, g * p), b_blk, c_blk


def _ssm(u, x0_re, x0_im, params, d_skip, w_glu, tile):
    n, t, w = u.shape
    lam_re, lam_im, b_blk, c_blk = params
    ns = lam_re.shape[1]
    x0_re = x0_re.reshape(n, 1, ns)
    x0_im = x0_im.reshape(n, 1, ns)
    seq = lambda width: pl.BlockSpec((1, tile, width), lambda b, j: (b, j, 0))
    state = pl.BlockSpec((1, 1, ns), lambda b, j: (b, 0, 0))
    y, sr, si = pl.pallas_call(
        functools.partial(_ssm_kernel, tile=tile), grid=(n, t // tile),
        in_specs=[seq(w), state, state, _full((1, ns)), _full((1, ns)), _full(b_blk.shape), _full(c_blk.shape),
                  _full((1, w)), _full(w_glu.shape)],
        out_specs=[seq(w), state, state],
        out_shape=[jax.ShapeDtypeStruct((n, t, w), BF16), jax.ShapeDtypeStruct((n, 1, ns), F32),
                   jax.ShapeDtypeStruct((n, 1, ns), F32)],
        scratch_shapes=[pltpu.VMEM((tile, 2 * ns), F32), pltpu.VMEM((SUBLANES, ns), F32)],
        compiler_params=_params("parallel", "arbitrary"), name="ssm",
    )(u, x0_re, x0_im, lam_re, lam_im, b_blk, c_blk, d_skip.reshape(1, w), w_glu)
    return y, sr, si


def _merge_kernel(x_ref, attn_ref, ssm_ref, gate_ref, wua_ref, wus_ref, wo_ref, gx_ref, wxq_ref, xo_ref, qc_ref):
    d = x_ref.shape[-1]
    gate = gate_ref[...]
    mixed = gate[:, 0:d] * _dot(attn_ref[...], wua_ref[...]) + gate[:, d:2 * d] * _dot(ssm_ref[...], wus_ref[...])
    x = x_ref[...] + _dot(mixed.astype(BF16), wo_ref[...])
    xo_ref[...] = x
    qc_ref[...] = _dot(_rmsnorm(x, gx_ref[...]).astype(BF16), wxq_ref[...]).astype(BF16)


def _merge(x, attn, ssm, gates, w_up_attn, w_up_ssm, w_out, g_cross, w_xq, tm):
    r, d = x.shape
    rows = lambda w: pl.BlockSpec((tm, w), lambda i: (i, 0))
    weights = (w_up_attn, w_up_ssm, w_out, g_cross.reshape(1, d), w_xq)
    return pl.pallas_call(
        _merge_kernel, grid=(r // tm,),
        in_specs=[rows(d), rows(attn.shape[1]), rows(ssm.shape[1]), rows(gates.shape[1])]
        + [_full(w.shape) for w in weights],
        out_specs=[rows(d), rows(w_xq.shape[1])],
        out_shape=[jax.ShapeDtypeStruct((r, d), F32), jax.ShapeDtypeStruct((r, w_xq.shape[1]), BF16)],
        compiler_params=_params("parallel"), name="merge",
    )(x, attn, ssm, gates, *weights)


def _cross_kernel(q_ref, mk_ref, mv_ref, o_ref):
    q = q_ref[0]
    outs = []
    for h in range(X_HEADS):
        sl = slice(h * X_HEAD_DIM, (h + 1) * X_HEAD_DIM)
        s = _dot_nt(q[:, sl], mk_ref[0, :, sl].astype(BF16)) * (X_HEAD_DIM ** -0.5)
        s = s - s.max(axis=-1, keepdims=True)
        p = jnp.exp(s)
        p = (p / p.sum(axis=-1, keepdims=True)).astype(BF16)
        outs.append(_dot(p, mv_ref[0, :, sl].astype(BF16)))
    o_ref[0] = jnp.concatenate(outs, axis=-1).astype(BF16)


def _cross(q, mk, mv, tq):
    n, t, w = q.shape
    mem = pl.BlockSpec((1,) + mk.shape[1:], lambda b, i: (b, 0, 0))
    blk = pl.BlockSpec((1, tq, w), lambda b, i: (b, i, 0))
    return pl.pallas_call(
        _cross_kernel, grid=(n, t // tq), in_specs=[blk, mem, mem], out_specs=blk,
        out_shape=jax.ShapeDtypeStruct((n, t, w), BF16),
        compiler_params=_params("parallel", "parallel"), name="cross",
    )(q, mk, mv)


def _mlp_kernel(x_ref, oc_ref, wxo_ref, gm_ref, wup_ref, wdn_ref, gf_ref, o_ref, *, ff_chunk, final):
    x = x_ref[...] + _dot(oc_ref[...], wxo_ref[...])
    h = _rmsnorm(x, gm_ref[...]).astype(BF16)
    acc = jnp.zeros(x.shape, F32)
    for c in range(wup_ref.shape[1] // ff_chunk):
        sl = slice(c * ff_chunk, (c + 1) * ff_chunk)
        a = jnp.maximum(_dot(h, wup_ref[:, sl]), 0.0)
        acc = acc + _dot((a * a).astype(BF16), wdn_ref[sl, :])
    x = x + acc
    o_ref[...] = _rmsnorm(x, gf_ref[...]) if final else x


def _mlp(x, o_cross, w_xo, g_mlp, w_up, w_down, g_final, tm, final):
    r, d = x.shape
    rows = lambda w: pl.BlockSpec((tm, w), lambda i: (i, 0))
    weights = (w_xo, g_mlp.reshape(1, d), w_up, w_down, g_final.reshape(1, d))
    return pl.pallas_call(
        functools.partial(_mlp_kernel, ff_chunk=min(1024, w_up.shape[1]), final=final), grid=(r // tm,),
        in_specs=[rows(d), rows(o_cross.shape[1])] + [_full(w.shape) for w in weights],
        out_specs=rows(d), out_shape=jax.ShapeDtypeStruct((r, d), F32),
        compiler_params=_params("parallel"), name="mlp",
    )(x, o_cross, *weights)


def _memkv_kernel(mem_ref, g_ref, wk_ref, wv_ref, mk_ref, mv_ref):
    hm = _rmsnorm(mem_ref[0], g_ref[...]).astype(BF16)
    mk_ref[0] = _dot(hm, wk_ref[...])
    mv_ref[0] = _dot(hm, wv_ref[...])


def _memkv(mem, g, w_k, w_v):
    n, m, d = mem.shape
    w = w_k.shape[1]
    blk = lambda width: pl.BlockSpec((1, m, width), lambda b: (b, 0, 0))
    return pl.pallas_call(
        _memkv_kernel, grid=(n,),
        in_specs=[blk(d), _full((1, d)), _full(w_k.shape), _full(w_v.shape)],
        out_specs=[blk(w), blk(w)], out_shape=[jax.ShapeDtypeStruct((n, m, w), F32)] * 2,
        compiler_params=_params("parallel"), name="memkv",
    )(mem, g.reshape(1, d), w_k, w_v)


PROMPT_ROW_TILE = 256
PROMPT_Q_TILE = 128
SSM_TIME_TILE = 256


def kernel(x_prompt, x_sample, mem_prompt, cache_k, cache_v, cache_idx_k, state_ssm_re, state_ssm_im,
           cache_mem_k, cache_mem_v, page_table, rel_bias, norm_mix, w_in, ssm_a_re, ssm_a_im,
           ssm_b_re, ssm_b_im, ssm_c_re, ssm_c_im, ssm_d, ssm_log_dt, w_glu, w_up_attn, w_up_ssm, w_out,
           norm_cross, norm_mem, w_xq, w_xk, w_xv, w_xo, norm_mlp, w_mlp_up, w_mlp_down, norm_final):
    depth = w_in.shape[0]
    nb, seq, d = x_prompt.shape
    ns, t_new, _ = x_sample.shape
    groups, state = ssm_a_re.shape[1], ssm_a_re.shape[2]
    n_mem = mem_prompt.shape[1]
    tm = _row_tile(seq, PROMPT_ROW_TILE)
    tq = _row_tile(seq, PROMPT_Q_TILE)
    ts = _row_tile(seq, SSM_TIME_TILE)
    bf = lambda a: a.astype(BF16)

    bias_prompt, bias_last, bias_new = _bias_tables(rel_bias, tq, t_new)
    yp = x_prompt.reshape(nb * seq, d)
    ys = x_sample.reshape(ns * t_new, d)
    zero_state = jnp.zeros((nb, groups * state), F32)
    outs = {name: [] for name in ("kp", "vp", "ikp", "srp", "sip", "mkp", "mvp", "ks", "vs", "iks", "srs", "sis")}

    for l in range(depth):
        ws = _split_w_in(w_in[l])
        ssm_params = _ssm_params(ssm_a_re[l], ssm_a_im[l], ssm_b_re[l], ssm_b_im[l], ssm_c_re[l], ssm_c_im[l],
                                 ssm_log_dt[l])
        dense = (bf(w_up_attn[l]), bf(w_up_ssm[l]), bf(w_out[l]), norm_cross[l], bf(w_xq[l]))
        mlp_w = (bf(w_xo[l]), norm_mlp[l], bf(w_mlp_up[l]), bf(w_mlp_down[l]), norm_final)
        glu = bf(w_glu[l])

        qT, iqT, iwT, k, v, ik, kb, vT, ikb, u, gates = _proj_prompt(yp.reshape(nb, seq, d), norm_mix[l], ws, tm)
        attn = _dsa_prompt(qT, iqT, iwT, ikb, kb, vT, bias_prompt, tq)
        ssm, s_re, s_im = _ssm(u, zero_state, zero_state, ssm_params, ssm_d[l], glu, ts)
        yp, qc = _merge(yp, attn.reshape(nb * seq, -1), ssm.reshape(nb * seq, -1), gates.reshape(nb * seq, -1),
                        *dense, tm)
        mk, mv = _memkv(mem_prompt, norm_mem[l], bf(w_xk[l]), bf(w_xv[l]))
        oc = _cross(qc.reshape(nb, seq, -1), mk, mv, tm)
        yp = _mlp(yp, oc.reshape(nb * seq, -1), *mlp_w, tm, l == depth - 1)
        outs["kp"].append(k.reshape(nb, seq, N_HEADS, HEAD_DIM))
        outs["vp"].append(v.reshape(nb, seq, N_HEADS, HEAD_DIM))
        outs["ikp"].append(ik)
        outs["srp"].append(s_re.reshape(nb, groups, state))
        outs["sip"].append(s_im.reshape(nb, groups, state))
        outs["mkp"].append(mk.reshape(nb, n_mem, X_HEADS, X_HEAD_DIM))
        outs["mvp"].append(mv.reshape(nb, n_mem, X_HEADS, X_HEAD_DIM))

        q, iq, iw, k, v, ik, u, gates = _proj_rows(ys, norm_mix[l], ws)
        r3 = lambda a: a.reshape(ns, t_new, -1)
        attn = _dsa_sample(r3(q), r3(iq), r3(iw), r3(k), r3(v), r3(ik), cache_k[l], cache_v[l], cache_idx_k[l],
                           page_table, bias_last, bias_new)
        ssm, s_re, s_im = _ssm(r3(u), state_ssm_re[l].reshape(ns, -1), state_ssm_im[l].reshape(ns, -1),
                               ssm_params, ssm_d[l], glu, t_new)
        ys, qc = _merge(ys, attn.reshape(ns * t_new, -1), ssm.reshape(ns * t_new, -1), gates, *dense, ns * t_new)
        oc = _cross(r3(qc), cache_mem_k[l].reshape(ns, n_mem, -1), cache_mem_v[l].reshape(ns, n_mem, -1), t_new)
        ys = _mlp(ys, oc.reshape(ns * t_new, -1), *mlp_w, ns * t_new, l == depth - 1)
        outs["ks"].append(k.reshape(ns, t_new, N_HEADS, HEAD_DIM))
        outs["vs"].append(v.reshape(ns, t_new, N_HEADS, HEAD_DIM))
        outs["iks"].append(ik.reshape(ns, t_new, IDX_DIM))
        outs["srs"].append(s_re.reshape(ns, groups, state))
        outs["sis"].append(s_im.reshape(ns, groups, state))

    st = lambda name: jnp.stack(outs[name])
    return (yp.reshape(nb, seq, d), ys.reshape(ns, t_new, d),
            st("kp"), st("vp"), st("ikp"), st("srp"), st("sip"), st("mkp"), st("mvp"),
            st("ks"), st("vs"), st("iks"), st("srs"), st("sis"))
```

```python
import functools
import math

import jax
import jax.numpy as jnp
import numpy as np
from jax import lax
from jax.experimental import pallas as pl
from jax.experimental.pallas import tpu as pltpu

F32 = jnp.float32
BF16 = jnp.bfloat16
I32 = jnp.int32

EPS = 1e-6
N_HEADS = 8
HEAD_DIM = 64
ATTN_WIDTH = N_HEADS * HEAD_DIM
IDX_HEADS = 8
IDX_DIM = 64
TOP_K_MAX = 256
SSM_GROUP = 16
SSM_STATE = 64
X_HEADS = 4
X_HEAD_DIM = 128
N_BUCKETS = 32
MAX_DISTANCE = 128
PAGE_SIZE = 128
V_ROWS = HEAD_DIM + 16

LANES = 128
SUBLANES = 8
VMEM_LIMIT_BYTES = 56 * 1024 * 1024
INT_MIN = -(2 ** 31)
INT_MAX = 2 ** 31 - 1
F32_MIN = float(np.finfo(np.float32).min)
F32_MAX = float(np.finfo(np.float32).max)
M_INIT = -1e30

NT_DIMS = (((1,), (1,)), ((), ()))


def _params(*semantics):
    return pltpu.CompilerParams(dimension_semantics=semantics, vmem_limit_bytes=VMEM_LIMIT_BYTES)


def _rmsnorm(x, g):
    ms = jnp.mean(x * x, axis=-1, keepdims=True)
    return x * lax.rsqrt(ms + EPS) * g


def _dot(a, b):
    return jnp.dot(a, b, preferred_element_type=F32)


def _dot_nt(a, b):
    return lax.dot_general(a, b, NT_DIMS, preferred_element_type=F32)


def _monotone_key(x):
    b = lax.bitcast_convert_type(x, I32)
    key = jnp.where(b < 0, b ^ INT_MAX, b)
    return jnp.where(x == 0.0, 0, key)


def _row_tile(rows, want):
    t = min(rows, want)
    assert rows % t == 0, (rows, t)
    return t


def _t5_bucket(dist):
    max_exact = N_BUCKETS // 2
    d = jnp.maximum(dist, 0)
    df = jnp.maximum(d, 1).astype(F32)
    large = max_exact + (jnp.log(df / max_exact) / math.log(MAX_DISTANCE / max_exact)
                         * (N_BUCKETS - max_exact)).astype(I32)
    large = jnp.minimum(large, N_BUCKETS - 1)
    return jnp.where(d < max_exact, d, large)


def _bias_lookup(rel_ref, bucket, h):
    out = jnp.zeros(bucket.shape, F32)
    for b in range(N_BUCKETS):
        out = jnp.where(bucket == b, rel_ref[b, h], out)
    return out - rel_ref[N_BUCKETS - 1, h]


def _bias_prompt_kernel(rel_ref, o_ref, *, tq):
    r = lax.broadcasted_iota(I32, (2 * tq, tq), 0)
    j = lax.broadcasted_iota(I32, (2 * tq, tq), 1)
    bucket = _t5_bucket(j - r + tq)
    for h in range(N_HEADS):
        o_ref[h] = _bias_lookup(rel_ref, bucket, h)


def _bias_sample_kernel(rel_ref, last_ref, new_ref, *, t_new):
    row = lax.broadcasted_iota(I32, (N_HEADS * t_new, PAGE_SIZE), 0)
    lane = lax.broadcasted_iota(I32, (N_HEADS * t_new, PAGE_SIZE), 1)
    last = jnp.zeros(row.shape, F32)
    new = jnp.zeros(row.shape, F32)
    for h in range(N_HEADS):
        t = row - h * t_new
        in_head = (t >= 0) & (t < t_new)
        last = jnp.where(in_head, _bias_lookup(rel_ref, _t5_bucket(PAGE_SIZE + t - lane), h), last)
        new = jnp.where(in_head, _bias_lookup(rel_ref, _t5_bucket(t - lane), h), new)
    last_ref[...] = last
    new_ref[...] = new


def _bias_tables(rel_bias, tq, t_new):
    smem = pl.BlockSpec(memory_space=pltpu.SMEM)
    prompt = pl.pallas_call(
        functools.partial(_bias_prompt_kernel, tq=tq),
        out_shape=jax.ShapeDtypeStruct((N_HEADS, 2 * tq, tq), F32),
        in_specs=[smem], name="bias_prompt")(rel_bias)
    last, new = pl.pallas_call(
        functools.partial(_bias_sample_kernel, t_new=t_new),
        out_shape=[jax.ShapeDtypeStruct((N_HEADS * t_new, PAGE_SIZE), F32)] * 2,
        in_specs=[smem], name="bias_sample")(rel_bias)
    return prompt, last, new


def _proj_prompt_kernel(x_ref, g_ref, wqT_ref, wiqT_ref, wiwT_ref, wk_ref, wkT_ref, wvT_ref, wik_ref, wikT_ref,
                        wu_ref, wg_ref,
                        qT_ref, iqw_ref, iwT_ref, kT_ref, vT_ref, ikT_ref, kb_ref, vTb_ref, ikb_ref, u_ref, gate_ref,
                        *, tq):
    h = _rmsnorm(x_ref[0], g_ref[...]).astype(BF16)
    qT_ref[0] = _dot_nt(wqT_ref[...], h).astype(BF16)
    iqT = _dot_nt(wiqT_ref[...], h).astype(BF16)
    for blk in range(iqw_ref.shape[1]):
        for hh in range(IDX_HEADS):
            iqw_ref[0, blk, :, hh * tq:(hh + 1) * tq] = iqT[hh * IDX_DIM:(hh + 1) * IDX_DIM, blk * tq:(blk + 1) * tq]
    iwT_ref[0] = _dot_nt(wiwT_ref[...], h) * (IDX_HEADS ** -0.5)
    kT_ref[0] = _dot_nt(wkT_ref[...], h)
    kb_ref[0] = _dot(h, wk_ref[...]).astype(BF16)
    vT = _dot_nt(wvT_ref[...], h)
    vT_ref[0] = vT
    ones = jnp.ones((V_ROWS - HEAD_DIM, vT.shape[1]), BF16)
    for hh in range(N_HEADS):
        vTb_ref[0, hh * V_ROWS:hh * V_ROWS + HEAD_DIM, :] = vT[hh * HEAD_DIM:(hh + 1) * HEAD_DIM, :].astype(BF16)
        vTb_ref[0, hh * V_ROWS + HEAD_DIM:(hh + 1) * V_ROWS, :] = ones
    ikT_ref[0] = _dot_nt(wikT_ref[...], h)
    ikb_ref[0] = _dot(h, wik_ref[...]).astype(BF16)
    u_ref[0] = _dot(h, wu_ref[...])
    gate_ref[0] = jax.nn.sigmoid(_dot(h, wg_ref[...]))


def _proj_rows_kernel(x_ref, g_ref, wq_ref, wiq_ref, wiw_ref, wk_ref, wv_ref, wik_ref, wu_ref, wg_ref,
                      q_ref, iq_ref, iw_ref, k_ref, v_ref, ik_ref, u_ref, gate_ref):
    h = _rmsnorm(x_ref[...], g_ref[...]).astype(BF16)
    q_ref[...] = _dot(h, wq_ref[...])
    iq_ref[...] = _dot(h, wiq_ref[...])
    iw_ref[...] = _dot(h, wiw_ref[...]) * (IDX_HEADS ** -0.5)
    k_ref[...] = _dot(h, wk_ref[...])
    v_ref[...] = _dot(h, wv_ref[...])
    ik_ref[...] = _dot(h, wik_ref[...])
    u_ref[...] = _dot(h, wu_ref[...])
    gate_ref[...] = jax.nn.sigmoid(_dot(h, wg_ref[...]))


def _split_w_in(w_in):
    sizes = (ATTN_WIDTH, ATTN_WIDTH, ATTN_WIDTH, IDX_HEADS * IDX_DIM, IDX_DIM, IDX_HEADS)
    offs = np.cumsum((0,) + sizes)
    d_model = w_in.shape[0]
    ssm_width = (w_in.shape[1] - offs[-1] - 2 * d_model)
    wq, wk, wv, wiq, wik, wiw = (w_in[:, offs[i]:offs[i + 1]] for i in range(6))
    wu = w_in[:, offs[-1]:offs[-1] + ssm_width]
    wg = w_in[:, offs[-1] + ssm_width:]
    wq = wq * (HEAD_DIM ** -0.5)
    wiq = wiq * (IDX_DIM ** -0.5)
    return tuple(w.astype(BF16) for w in (wq, wk, wv, wiq, wik, wiw, wu, wg))


def _full(shape):
    return pl.BlockSpec(shape, lambda *_: (0,) * len(shape))


def _proj_prompt(x, g, ws, tm, tq):
    n, t, d = x.shape
    wq, wk, wv, wiq, wik, wiw, wu, wg = ws
    sw, gw = wu.shape[1], wg.shape[1]
    assert tm % tq == 0
    weights = (wq.T, wiq.T, wiw.T, wk, wk.T, wv.T, wik, wik.T, wu, wg)
    rows = lambda w: pl.BlockSpec((1, tm, w), lambda b, i: (b, i, 0))
    cols = lambda w: pl.BlockSpec((1, w, tm), lambda b, i: (b, 0, i))
    iq_wide = IDX_HEADS * tq
    out_shape = [
        jax.ShapeDtypeStruct((n, ATTN_WIDTH, t), BF16),
        jax.ShapeDtypeStruct((n, t // tq, IDX_DIM, iq_wide), BF16),
        jax.ShapeDtypeStruct((n, IDX_HEADS, t), F32),
        jax.ShapeDtypeStruct((n, ATTN_WIDTH, t), F32),
        jax.ShapeDtypeStruct((n, ATTN_WIDTH, t), F32),
        jax.ShapeDtypeStruct((n, IDX_DIM, t), F32),
        jax.ShapeDtypeStruct((n, t, ATTN_WIDTH), BF16),
        jax.ShapeDtypeStruct((n, N_HEADS * V_ROWS, t), BF16),
        jax.ShapeDtypeStruct((n, t, IDX_DIM), BF16),
        jax.ShapeDtypeStruct((n, t, sw), F32),
        jax.ShapeDtypeStruct((n, t, gw), F32),
    ]
    out_specs = [cols(ATTN_WIDTH), pl.BlockSpec((1, tm // tq, IDX_DIM, iq_wide), lambda b, i: (b, i, 0, 0)),
                 cols(IDX_HEADS), cols(ATTN_WIDTH), cols(ATTN_WIDTH), cols(IDX_DIM), rows(ATTN_WIDTH),
                 cols(N_HEADS * V_ROWS), rows(IDX_DIM), rows(sw), rows(gw)]
    return pl.pallas_call(
        functools.partial(_proj_prompt_kernel, tq=tq), grid=(n, t // tm), out_shape=out_shape,
        in_specs=[rows(d), _full((1, d))] + [_full(w.shape) for w in weights],
        out_specs=out_specs, compiler_params=_params("parallel", "parallel"), name="proj_prompt",
    )(x, g.reshape(1, d), *weights)


def _proj_rows(x, g, ws):
    r, d = x.shape
    wq, wk, wv, wiq, wik, wiw, wu, wg = ws
    weights = (wq, wiq, wiw, wk, wv, wik, wu, wg)
    out_shape = [jax.ShapeDtypeStruct((r, w.shape[1]), F32) for w in weights]
    return pl.pallas_call(
        _proj_rows_kernel, out_shape=out_shape, compiler_params=_params(), name="proj_rows",
    )(x, g.reshape(1, d), *weights)


def _dsa_prompt_kernel(qT_ref, iqw_ref, iwT_ref, ikb_ref, kb_ref, vT_ref, bias_ref, o_ref,
                       s_ref, pen_ref, pnear_ref, q2_ref, sbuf_ref, pbuf_ref, acc_ref, m_ref, l_ref, j_ref,
                       *, tq, top_k, idx_bits):
    i = pl.program_id(1)
    gk = KEY_GROUP * tq
    n_grp = lax.div(i + KEY_GROUP, KEY_GROUP)
    n_far_grp = lax.div(jnp.maximum(i - 1, 0) + KEY_GROUP - 1, KEY_GROUP)
    iw = iwT_ref[0]
    row = lax.broadcasted_iota(I32, (tq, tq), 0)
    col = lax.broadcasted_iota(I32, (tq, tq), 1)
    grow = lax.broadcasted_iota(I32, (gk, tq), 0)

    ahead = row - col

    def score_group(g, carry):
        for cc in range(KEY_GROUP):
            c = g * KEY_GROUP + cc
            r0 = pl.multiple_of(c * tq, tq)
            d = _dot(ikb_ref[0, pl.ds(r0, tq), :], iqw_ref[0, 0])
            sc = jnp.zeros((tq, tq), F32)
            for h in range(IDX_HEADS):
                sc = sc + iw[h:h + 1, :] * jnp.maximum(d[:, h * tq:(h + 1) * tq], 0.0)
            s_ref[pl.ds(r0, tq), :] = jnp.where(ahead > (i - c) * tq, INT_MIN, _monotone_key(sc))
        return carry

    lax.fori_loop(0, n_grp, score_group, 0)

    def count(pred):
        def body(g, acc8):
            r0 = pl.multiple_of(g * gk, gk)
            ind = jnp.where(pred(s_ref[pl.ds(r0, gk), :], r0 + grow), 1, 0)
            return acc8 + ind.reshape(gk // SUBLANES, SUBLANES, tq).sum(axis=0)
        acc8 = lax.fori_loop(0, n_grp, body, jnp.zeros((SUBLANES, tq), I32))
        return acc8.sum(axis=0, keepdims=True)

    def bit_step(it, tu):
        cand_u = tu | lax.shift_left(jnp.int32(1), 31 - it)
        cand = cand_u ^ INT_MIN
        cnt = count(lambda blk, r0: blk >= cand)
        return jnp.where(cnt >= top_k, cand_u, tu)

    tu = lax.fori_loop(0, 32, bit_step, jnp.zeros((1, tq), I32))
    thr = jnp.maximum(tu ^ INT_MIN, INT_MIN + 1)
    cnt_gt = count(lambda blk, r0: blk > thr)
    cnt_ge = count(lambda blk, r0: blk >= thr)
    need = top_k - cnt_gt
    multi = cnt_ge > top_k

    j_ref[...] = jnp.full(j_ref.shape, INT_MAX, I32)

    @pl.when(jnp.max(multi.astype(I32)) > 0)
    def _():
        def j_step(it, jv):
            cand = jv | lax.shift_left(jnp.int32(1), idx_bits - 1 - it)
            cnt = count(lambda blk, ridx: (blk == thr) & (ridx < cand))
            return jnp.where(cnt < need, cand, jv)
        jv = lax.fori_loop(0, idx_bits, j_step, jnp.zeros((1, tq), I32))
        j_ref[0:1, :] = jnp.where(multi, jv, INT_MAX)

    jsel = j_ref[0:1, :]

    def selected(blk, ridx):
        return (blk > thr) | ((blk == thr) & (ridx <= jsel))

    def pen_group(g, carry):
        r0 = pl.multiple_of(g * gk, gk)
        ridx = r0 + grow
        sel = selected(s_ref[pl.ds(r0, gk), :], ridx) & (ridx < (i - 1) * tq)
        pen_ref[pl.ds(r0, gk), :] = jnp.where(sel, 0.0, F32_MIN)
        return carry

    lax.fori_loop(0, jnp.maximum(n_far_grp, 1), pen_group, 0)
    w0 =pl.multiple_of(jnp.maximum(i - 1, 0) * tq, tq)
    pnear_ref[...] = jnp.where(selected(s_ref[pl.ds(w0, 2 * tq), :], w0 + grow[0:2 * tq]), 0.0, F32_MIN)

    zero = jnp.zeros((HEAD_DIM, tq), BF16)
    for hp in range(N_HEADS // 2):
        lo = qT_ref[0, (2 * hp) * HEAD_DIM:(2 * hp + 1) * HEAD_DIM, :]
        hi = qT_ref[0, (2 * hp + 1) * HEAD_DIM:(2 * hp + 2) * HEAD_DIM, :]
        q2_ref[hp * 2 * HEAD_DIM:(hp + 1) * 2 * HEAD_DIM, :] = jnp.concatenate(
            [jnp.concatenate([lo, zero], axis=1), jnp.concatenate([zero, hi], axis=1)], axis=0)
    m_ref[...] = jnp.full(m_ref.shape, M_INIT, F32)
    l_ref[...] = jnp.zeros(l_ref.shape, F32)
    acc_ref[...] = jnp.zeros(acc_ref.shape, F32)

    def update_head(h, m8, r0, n_c):
        hs = slice(h * HEAD_DIM, (h + 1) * HEAD_DIM)
        ls = slice(h * SUBLANES, (h + 1) * SUBLANES)
        m_old = m_ref[h:h + 1, :]
        m_new = jnp.maximum(m_old, m8.max(axis=0, keepdims=True))
        alpha = jnp.exp(m_old - m_new)
        for cc in range(n_c):
            p = jnp.exp(sbuf_ref[h, cc * tq:(cc + 1) * tq, :] - m_new)
            pbuf_ref[h, cc * tq:(cc + 1) * tq, :] = p.astype(BF16)
        pv = _dot(vT_ref[0, h * V_ROWS:(h + 1) * V_ROWS, pl.ds(r0, n_c * tq)], pbuf_ref[h, 0:n_c * tq, :])
        acc_ref[hs, :] = alpha * acc_ref[hs, :] + pv[0:HEAD_DIM, :]
        l_ref[ls, :] = alpha * l_ref[ls, :] + pv[HEAD_DIM:HEAD_DIM + SUBLANES, :]
        m_ref[h:h + 1, :] = m_new

    def logits(hp, r0, n_c, pen_src, pen_r0, bias_rows):
        m8 = [None, None]
        for cc in range(n_c):
            pen = pen_src[pl.ds(pen_r0 + cc * tq, tq), :]
            kslab = kb_ref[0, pl.ds(r0 + cc * tq, tq), hp * 2 * HEAD_DIM:(hp + 1) * 2 * HEAD_DIM]
            s2 = _dot(kslab, q2_ref[hp * 2 * HEAD_DIM:(hp + 1) * 2 * HEAD_DIM, :])
            for e in range(2):
                h = 2 * hp + e
                s = s2[:, e * tq:(e + 1) * tq] + pen
                if bias_rows is not None:
                    s = s + bias_ref[h, pl.ds(bias_rows[cc], tq), :]
                sbuf_ref[h, cc * tq:(cc + 1) * tq, :] = s
                cm = s.reshape(tq // SUBLANES, SUBLANES, tq).max(axis=0)
                m8[e] = cm if m8[e] is None else jnp.maximum(m8[e], cm)
        return tuple(m8)

    def attend_group(m8, r0, n_c, pen_src, pen_r0, bias_rows, next_group_logits):
        pairs = N_HEADS // 2
        for hp in range(pairs):
            if hp + 1 < pairs:
                m8_next = logits(hp + 1, r0, n_c, pen_src, pen_r0, bias_rows)
            else:
                m8_next = next_group_logits() if next_group_logits is not None else None
            for e in range(2):
                update_head(2 * hp + e, m8[e], r0, n_c)
            m8 = m8_next
        return m8

    def far_logits0(g):
        r0 = pl.multiple_of(g * gk, gk)
        return logits(0, r0, KEY_GROUP, pen_ref, r0, None)

    def far_group(g, m8):
        r0 = pl.multiple_of(g * gk, gk)
        nxt = jnp.minimum(g + 1, n_far_grp - 1)
        return attend_group(m8, r0, KEY_GROUP, pen_ref, r0, None, lambda: far_logits0(nxt))

    lax.fori_loop(0, n_far_grp, far_group, far_logits0(0))
    first_bias = pl.multiple_of(jnp.where(i == 0, tq, 0), tq)
    near = (w0, 2, pnear_ref, 0, (first_bias, tq))
    attend_group(logits(0, *near), *near, None)

    for h in range(N_HEADS):
        hs = slice(h * HEAD_DIM, (h + 1) * HEAD_DIM)
        acc_ref[hs, :] = acc_ref[hs, :] / l_ref[h * SUBLANES:h * SUBLANES + 1, :]
    o_ref[0] = acc_ref[...].T.astype(BF16)


def _dsa_prompt(qT, iqw, iwT, ikb, kb, vT, bias, tq):
    n, _, t = qT.shape
    n_blk = t // tq
    assert t % tq == 0 and tq >= MAX_DISTANCE and n_blk % KEY_GROUP == 0 and n_blk >= 2, (t, tq)
    top_k = min(TOP_K_MAX, t // 4)
    idx_bits = max(1, int(math.ceil(math.log2(t))))
    gk = KEY_GROUP * tq
    colblk = lambda w: pl.BlockSpec((1, w, tq), lambda b, i: (b, 0, i))
    whole = lambda a: pl.BlockSpec((1,) + a.shape[1:], lambda b, i: (b, 0, 0), pipeline_mode=pl.Buffered(1))
    kern = functools.partial(_dsa_prompt_kernel, tq=tq, top_k=top_k, idx_bits=idx_bits)
    return pl.pallas_call(
        kern, grid=(n, n_blk),
        out_shape=jax.ShapeDtypeStruct((n, t, ATTN_WIDTH), BF16),
        in_specs=[colblk(ATTN_WIDTH), pl.BlockSpec((1, 1) + iqw.shape[2:], lambda b, i: (b, i, 0, 0)),
                  colblk(IDX_HEADS), whole(ikb), whole(kb), whole(vT),
                  pl.BlockSpec(bias.shape, lambda b, i: (0, 0, 0), pipeline_mode=pl.Buffered(1))],
        out_specs=pl.BlockSpec((1, tq, ATTN_WIDTH), lambda b, i: (b, i, 0)),
        scratch_shapes=[pltpu.VMEM((t, tq), I32),
                        pltpu.VMEM((t, tq), F32),
                        pltpu.VMEM((2 * tq, tq), F32),
                        pltpu.VMEM((N_HEADS * HEAD_DIM, 2 * tq), BF16),
                        pltpu.VMEM((N_HEADS, gk, tq), F32),
                        pltpu.VMEM((N_HEADS, gk, tq), BF16),
                        pltpu.VMEM((ATTN_WIDTH, tq), F32),
                        pltpu.VMEM((N_HEADS, tq), F32),
                        pltpu.VMEM((N_HEADS * SUBLANES, tq), F32),
                        pltpu.VMEM((SUBLANES, tq), I32)],
        compiler_params=_params("parallel", "arbitrary"), name="dsa_prompt",
    )(qT, iqw, iwT, ikb, kb, vT, bias)


def _page_specs(page_shape, layer, pages_per_step):
    def spec(r):
        return pl.BlockSpec((1, 1) + page_shape,
                            lambda b, g, pt: (layer, pt[b, g * pages_per_step + r]) + (0,) * len(page_shape))
    return [spec(r) for r in range(pages_per_step)]


def _head_sum(x, t_new):
    return x.reshape(IDX_HEADS, t_new, x.shape[-1]).sum(axis=0)


def _dsa_sample_score_kernel(pt_ref, *refs, pps, t_new):
    page_refs = refs[:pps]
    iq_ref, w_ref, iknew_ref, sp_ref, sn_ref = refs[pps:]
    g = pl.program_id(1)
    iq = iq_ref[0]
    w = w_ref[0]
    for r in range(pps):
        d = _dot(iq, page_refs[r][0, 0].astype(BF16))
        sc = _head_sum(w * jnp.maximum(d, 0.0), t_new)
        sp_ref[0, :, r * PAGE_SIZE:(r + 1) * PAGE_SIZE] = _monotone_key(sc)

    @pl.when(g == pl.num_programs(1) - 1)
    def _():
        d = _dot_nt(iq, iknew_ref[0].astype(BF16))
        sc = _head_sum(w * jnp.maximum(d, 0.0), t_new)
        t = lax.broadcasted_iota(I32, sc.shape, 0)
        j = lax.broadcasted_iota(I32, sc.shape, 1)
        sn_ref[0] = jnp.where(j > t, INT_MIN, _monotone_key(sc))


def _dsa_sample_thr_kernel(sp_ref, sn_ref, thr_ref, j_ref, *, top_k, idx_bits):
    rows, past = sp_ref.shape
    lane = lax.broadcasted_iota(I32, (rows, LANES), 1)

    def count(pred):
        tot = jnp.where(pred(sn_ref[...], past + lane), 1, 0)
        for c in range(past // LANES):
            tot = tot + jnp.where(pred(sp_ref[:, c * LANES:(c + 1) * LANES], c * LANES + lane), 1, 0)
        return jnp.broadcast_to(tot.sum(axis=-1, keepdims=True), (rows, LANES))

    def bit_step(it, tu):
        cand_u = tu | lax.shift_left(jnp.int32(1), 31 - it)
        cand = cand_u ^ INT_MIN
        cnt = count(lambda blk, idx: blk >= cand)
        return jnp.where(cnt >= top_k, cand_u, tu)

    tu = lax.fori_loop(0, 32, bit_step, jnp.zeros((rows, LANES), I32))
    thr = jnp.maximum(tu ^ INT_MIN, INT_MIN + 1)
    cnt_gt = count(lambda blk, idx: blk > thr)
    cnt_ge = count(lambda blk, idx: blk >= thr)
    need = top_k - cnt_gt
    multi = cnt_ge > top_k
    thr_ref[...] = thr
    j_ref[...] = jnp.full((rows, LANES), INT_MAX, I32)

    @pl.when(jnp.max(multi.astype(I32)) > 0)
    def _():
        def j_step(it, jv):
            cand = jv | lax.shift_left(jnp.int32(1), idx_bits - 1 - it)
            cnt = count(lambda blk, idx: (blk == thr) & (idx < cand))
            return jnp.where(cnt < need, cand, jv)
        jv = lax.fori_loop(0, idx_bits, j_step, jnp.zeros((rows, LANES), I32))
        j_ref[...] = jnp.where(multi, jv, INT_MAX)


def _dsa_sample_attn_kernel(pt_ref, *refs, pps, t_new):
    k_refs = refs[:pps]
    v_refs = refs[pps:2 * pps]
    (q_ref, sp_ref, sn_ref, thr_ref, j_ref, knew_ref, vnew_ref, blast_ref, bnew_ref,
     o_ref, acc_ref, m_ref, l_ref) = refs[2 * pps:]
    g = pl.program_id(1)
    last = g == pl.num_programs(1) - 1
    thr = thr_ref[0]
    jsel = j_ref[0]
    lane = lax.broadcasted_iota(I32, (t_new, PAGE_SIZE), 1)

    @pl.when(g == 0)
    def _():
        m_ref[...] = jnp.full(m_ref.shape, M_INIT, F32)
        l_ref[...] = jnp.zeros(l_ref.shape, F32)
        acc_ref[...] = jnp.zeros(acc_ref.shape, F32)

    def update(pages):
        logit = []
        for keys, idx, kT_ref, _, bias in pages:
            s = jnp.concatenate([_dot(q_ref[0, h], kT_ref[h].astype(BF16)) for h in range(N_HEADS)], axis=0)
            if bias is not None:
                s = s + bias
            sel = (keys > thr) | ((keys == thr) & (idx <= jsel))
            cap = jnp.where(sel, F32_MAX, F32_MIN)
            logit.append(jnp.minimum(s, jnp.concatenate([cap] * N_HEADS, axis=0)))
        m_old = m_ref[...]
        m_new = jnp.maximum(m_old, jnp.broadcast_to(functools.reduce(jnp.maximum, logit).max(axis=-1, keepdims=True),
                                                    m_old.shape))
        prob = [jnp.exp(s - m_new) for s in logit]
        alpha = jnp.exp(m_old - m_new)
        l_ref[...] = alpha * l_ref[...] + jnp.broadcast_to(sum(prob).sum(axis=-1, keepdims=True), m_old.shape)
        pv = sum(jnp.concatenate([_dot_nt(p[h * t_new:(h + 1) * t_new, :].astype(BF16), page[3][h].astype(BF16))
                                  for h in range(N_HEADS)], axis=0)
                 for p, page in zip(prob, pages))
        acc_ref[...] = alpha[:, 0:HEAD_DIM] * acc_ref[...] + pv
        m_ref[...] = m_new

    last_f = jnp.where(last, 1.0, 0.0)
    update([(sp_ref[0, :, r * PAGE_SIZE:(r + 1) * PAGE_SIZE], (g * pps + r) * PAGE_SIZE + lane,
             k_refs[r].at[0, 0], v_refs[r].at[0, 0],
             blast_ref[...] * last_f if r == pps - 1 else None)
            for r in range(pps)])

    @pl.when(last)
    def _():
        n_past = pl.num_programs(1) * pps * PAGE_SIZE
        update([(sn_ref[0], n_past + lane, knew_ref.at[0], vnew_ref.at[0], bnew_ref[...])])
        o_full = acc_ref[...] / l_ref[:, 0:HEAD_DIM]
        for h in range(N_HEADS):
            o_ref[0, h] = o_full[h * t_new:(h + 1) * t_new, :].astype(BF16)


def _dsa_sample(q, iq, iw, k_new, v_new, ik_new, pool_kT, pool_vT, pool_ikT, layer, page_table, bias_last,
                bias_new):
    n, t_new, _ = q.shape
    n_pages = page_table.shape[1]
    past = n_pages * PAGE_SIZE
    top_k = min(TOP_K_MAX, (past + t_new) // 4)
    pps = math.gcd(n_pages, 8)
    groups = n_pages // pps
    ht = N_HEADS * t_new
    assert t_new <= PAGE_SIZE and IDX_HEADS == N_HEADS

    heads = lambda a, dim: a.reshape(n, t_new, N_HEADS, dim).transpose(0, 2, 1, 3)
    iq_hm = heads(iq, IDX_DIM).reshape(n, ht, IDX_DIM).astype(BF16)
    w_hm = jnp.broadcast_to(iw.transpose(0, 2, 1).reshape(n, ht, 1), (n, ht, LANES))
    q_hm = heads(q, HEAD_DIM).astype(BF16)
    page = lambda a: jnp.pad(heads(a, HEAD_DIM).transpose(0, 1, 3, 2),
                             ((0, 0), (0, 0), (0, 0), (0, PAGE_SIZE - t_new)))
    k_page, v_page = page(k_new), page(v_new)
    ik_pad = jnp.pad(ik_new, ((0, 0), (0, PAGE_SIZE - t_new), (0, 0)))

    per_b = lambda a: pl.BlockSpec((1,) + a.shape[1:], lambda b, g, pt: (b,) + (0,) * (a.ndim - 1))
    const = lambda a: pl.BlockSpec(a.shape, lambda b, g, pt: (0,) * a.ndim)
    sp_spec = pl.BlockSpec((1, t_new, pps * PAGE_SIZE), lambda b, g, pt: (b, 0, g))

    sp, sn = pl.pallas_call(
        functools.partial(_dsa_sample_score_kernel, pps=pps, t_new=t_new),
        grid_spec=pltpu.PrefetchScalarGridSpec(
            num_scalar_prefetch=1, grid=(n, groups),
            in_specs=_page_specs((IDX_DIM, PAGE_SIZE), layer, pps) + [per_b(iq_hm), per_b(w_hm), per_b(ik_pad)],
            out_specs=[sp_spec, pl.BlockSpec((1, t_new, PAGE_SIZE), lambda b, g, pt: (b, 0, 0))]),
        out_shape=[jax.ShapeDtypeStruct((n, t_new, past), I32), jax.ShapeDtypeStruct((n, t_new, PAGE_SIZE), I32)],
        compiler_params=_params("parallel", "arbitrary"), name="dsa_sample_score",
    )(page_table, *([pool_ikT] * pps), iq_hm, w_hm, ik_pad)

    rows = n * t_new
    rt = _row_tile(rows, 32)
    idx_bits = int(math.ceil(math.log2(past + PAGE_SIZE)))
    thr, jsel = pl.pallas_call(
        functools.partial(_dsa_sample_thr_kernel, top_k=top_k, idx_bits=idx_bits),
        grid=(rows // rt,),
        in_specs=[pl.BlockSpec((rt, past), lambda i: (i, 0)), pl.BlockSpec((rt, PAGE_SIZE), lambda i: (i, 0))],
        out_specs=[pl.BlockSpec((rt, LANES), lambda i: (i, 0))] * 2,
        out_shape=[jax.ShapeDtypeStruct((rows, LANES), I32)] * 2,
        compiler_params=_params("parallel"), name="dsa_sample_thr",
    )(sp.reshape(rows, past), sn.reshape(rows, PAGE_SIZE))
    thr = thr.reshape(n, t_new, LANES)
    jsel = jsel.reshape(n, t_new, LANES)

    kv_page = (N_HEADS, HEAD_DIM, PAGE_SIZE)
    o = pl.pallas_call(
        functools.partial(_dsa_sample_attn_kernel, pps=pps, t_new=t_new),
        grid_spec=pltpu.PrefetchScalarGridSpec(
            num_scalar_prefetch=1, grid=(n, groups),
            in_specs=(_page_specs(kv_page, layer, pps) + _page_specs(kv_page, layer, pps)
                      + [per_b(q_hm), sp_spec, per_b(sn), per_b(thr), per_b(jsel), per_b(k_page), per_b(v_page),
                         const(bias_last), const(bias_new)]),
            out_specs=pl.BlockSpec((1, N_HEADS, t_new, HEAD_DIM), lambda b, g, pt: (b, 0, 0, 0)),
            scratch_shapes=[pltpu.VMEM((ht, HEAD_DIM), F32), pltpu.VMEM((ht, LANES), F32),
                            pltpu.VMEM((ht, LANES), F32)]),
        out_shape=jax.ShapeDtypeStruct((n, N_HEADS, t_new, HEAD_DIM), BF16),
        compiler_params=_params("parallel", "arbitrary"), name="dsa_sample_attn",
    )(page_table, *([pool_kT] * pps), *([pool_vT] * pps), q_hm, sp, sn, thr, jsel, k_page, v_page, bias_last,
      bias_new)
    return o.transpose(0, 2, 1, 3).reshape(n, t_new, ATTN_WIDTH)


def _gelu_tanh(y):
    return 0.5 * y * (1.0 + jnp.tanh(math.sqrt(2.0 / math.pi) * (y + 0.044715 * (y * y * y))))


def _ssm_kernel(u_ref, x0r_ref, x0i_ref, lr_ref, li_ref, b_ref, c_ref, d_ref, wglu_ref,
                y_ref, sr_ref, si_ref, x_ref, st_ref, *, tile):
    j = pl.program_id(1)
    ns = lr_ref.shape[1]

    @pl.when(j == 0)
    def _():
        st_ref[0:1, :] = x0r_ref[0]
        st_ref[1:2, :] = x0i_ref[0]

    u = u_ref[0]
    x_ref[...] = _dot(u.astype(BF16), b_ref[...])
    lr = lr_ref[...]
    li = li_ref[...]

    def step(t, carry):
        sr, si = carry
        br = x_ref[pl.ds(t, 1), 0:ns]
        bi = x_ref[pl.ds(t, 1), ns:2 * ns]
        nr = lr * sr - li * si + br
        ni = lr * si + li * sr + bi
        x_ref[pl.ds(t, 1), 0:ns] = nr
        x_ref[pl.ds(t, 1), ns:2 * ns] = ni
        return nr, ni

    sr, si = lax.fori_loop(0, tile, step, (st_ref[0:1, :], st_ref[1:2, :]))
    st_ref[0:1, :] = sr
    st_ref[1:2, :] = si

    y = _dot(x_ref[...].astype(BF16), c_ref[...]) + d_ref[...] * u
    y = _gelu_tanh(y)
    y_ref[0] = (y * jax.nn.sigmoid(_dot(y.astype(BF16), wglu_ref[...]))).astype(BF16)

    @pl.when(j == pl.num_programs(1) - 1)
    def _():
        sr_ref[0] = sr
        si_ref[0] = si


def _ssm_params(a_re, a_im, b_re, b_im, c_re, c_im, log_dt):
    g = a_re.shape[0]
    dt = jnp.exp(log_dt)[:, None]
    mag = jnp.exp(a_re * dt)
    lam_re, lam_im = mag * jnp.cos(a_im * dt), mag * jnp.sin(a_im * dt)
    den = a_re * a_re + a_im * a_im
    nr, ni = lam_re - 1.0, lam_im
    f_re = (nr * a_re + ni * a_im) / den
    f_im = (ni * a_re - nr * a_im) / den
    bb_re = f_re[..., None] * b_re - f_im[..., None] * b_im
    bb_im = f_re[..., None] * b_im + f_im[..., None] * b_re
    eye = jnp.eye(g, dtype=F32)
    p, c = bb_re.shape[1], bb_re.shape[2]
    blk_in = lambda m: jnp.einsum('gpc,gh->gchp', m, eye).reshape(g * c, g * p)
    blk_out = lambda m: jnp.einsum('gcp,gh->gphc', m, eye).reshape(g * p, g * c)
    b_blk = jnp.concatenate([blk_in(bb_re), blk_in(bb_im)], axis=1).astype(BF16)
    c_blk = jnp.concatenate([blk_out(c_re), -blk_out(c_im)], axis=0).astype(BF16)
    return lam_re.reshape(1, g * p), lam_im.reshape(1, g * p), b_blk, c_blk


def _ssm(u, x0_re, x0_im, params, d_skip, w_glu, tile):
    n, t, w = u.shape
    lam_re, lam_im, b_blk, c_blk = params
    ns = lam_re.shape[1]
    x0_re = x0_re.reshape(n, 1, ns)
    x0_im = x0_im.reshape(n, 1, ns)
    seq = lambda width: pl.BlockSpec((1, tile, width), lambda b, j: (b, j, 0))
    state = pl.BlockSpec((1, 1, ns), lambda b, j: (b, 0, 0))
    y, sr, si = pl.pallas_call(
        functools.partial(_ssm_kernel, tile=tile), grid=(n, t // tile),
        in_specs=[seq(w), state, state, _full((1, ns)), _full((1, ns)), _full(b_blk.shape), _full(c_blk.shape),
                  _full((1, w)), _full(w_glu.shape)],
        out_specs=[seq(w), state, state],
        out_shape=[jax.ShapeDtypeStruct((n, t, w), BF16), jax.ShapeDtypeStruct((n, 1, ns), F32),
                   jax.ShapeDtypeStruct((n, 1, ns), F32)],
        scratch_shapes=[pltpu.VMEM((tile, 2 * ns), F32), pltpu.VMEM((SUBLANES, ns), F32)],
        compiler_params=_params("parallel", "arbitrary"), name="ssm",
    )(u, x0_re, x0_im, lam_re, lam_im, b_blk, c_blk, d_skip.reshape(1, w), w_glu)
    return y, sr, si


def _merge_kernel(x_ref, attn_ref, ssm_ref, gate_ref, wua_ref, wus_ref, wo_ref, gx_ref, wxq_ref, xo_ref, qc_ref):
    d = x_ref.shape[-1]
    gate = gate_ref[...]
    mixed = gate[:, 0:d] * _dot(attn_ref[...], wua_ref[...]) + gate[:, d:2 * d] * _dot(ssm_ref[...], wus_ref[...])
    x = x_ref[...] + _dot(mixed.astype(BF16), wo_ref[...])
    xo_ref[...] = x
    qc_ref[...] = _dot(_rmsnorm(x, gx_ref[...]).astype(BF16), wxq_ref[...]).astype(BF16)


def _merge(x, attn, ssm, gates, w_up_attn, w_up_ssm, w_out, g_cross, w_xq, tm):
    r, d = x.shape
    rows = lambda w: pl.BlockSpec((tm, w), lambda i: (i, 0))
    weights = (w_up_attn, w_up_ssm, w_out, g_cross.reshape(1, d), w_xq)
    return pl.pallas_call(
        _merge_kernel, grid=(r // tm,),
        in_specs=[rows(d), rows(attn.shape[1]), rows(ssm.shape[1]), rows(gates.shape[1])]
        + [_full(w.shape) for w in weights],
        out_specs=[rows(d), rows(w_xq.shape[1])],
        out_shape=[jax.ShapeDtypeStruct((r, d), F32), jax.ShapeDtypeStruct((r, w_xq.shape[1]), BF16)],
        compiler_params=_params("parallel"), name="merge",
    )(x, attn, ssm, gates, *weights)


def _cross_kernel(q_ref, mk_ref, mv_ref, o_ref):
    q = q_ref[0]
    outs = []
    for h in range(X_HEADS):
        sl = slice(h * X_HEAD_DIM, (h + 1) * X_HEAD_DIM)
        s = _dot_nt(q[:, sl], mk_ref[0, 0, :, sl].astype(BF16)) * (X_HEAD_DIM ** -0.5)
        s = s - s.max(axis=-1, keepdims=True)
        p = jnp.exp(s)
        p = (p / p.sum(axis=-1, keepdims=True)).astype(BF16)
        outs.append(_dot(p, mv_ref[0, 0, :, sl].astype(BF16)))
    o_ref[0] = jnp.concatenate(outs, axis=-1).astype(BF16)


def _cross(q, mk, mv, layer, tq):
    n, t, w = q.shape
    mem = pl.BlockSpec((1, 1) + mk.shape[2:], lambda b, i: (layer, b, 0, 0))
    blk = pl.BlockSpec((1, tq, w), lambda b, i: (b, i, 0))
    return pl.pallas_call(
        _cross_kernel, grid=(n, t // tq), in_specs=[blk, mem, mem], out_specs=blk,
        out_shape=jax.ShapeDtypeStruct((n, t, w), BF16),
        compiler_params=_params("parallel", "parallel"), name="cross",
    )(q, mk, mv)


def _mlp_kernel(x_ref, oc_ref, wxo_ref, gm_ref, wup_ref, wdn_ref, gf_ref, o_ref, *, ff_chunk, final):
    x = x_ref[...] + _dot(oc_ref[...], wxo_ref[...])
    h = _rmsnorm(x, gm_ref[...]).astype(BF16)
    acc = jnp.zeros(x.shape, F32)
    for c in range(wup_ref.shape[1] // ff_chunk):
        sl = slice(c * ff_chunk, (c + 1) * ff_chunk)
        a = jnp.maximum(_dot(h, wup_ref[:, sl]), 0.0)
        acc = acc + _dot((a * a).astype(BF16), wdn_ref[sl, :])
    x = x + acc
    o_ref[...] = _rmsnorm(x, gf_ref[...]) if final else x


def _mlp(x, o_cross, w_xo, g_mlp, w_up, w_down, g_final, tm, final):
    r, d = x.shape
    rows = lambda w: pl.BlockSpec((tm, w), lambda i: (i, 0))
    weights = (w_xo, g_mlp.reshape(1, d), w_up, w_down, g_final.reshape(1, d))
    return pl.pallas_call(
        functools.partial(_mlp_kernel, ff_chunk=min(1024, w_up.shape[1]), final=final), grid=(r // tm,),
        in_specs=[rows(d), rows(o_cross.shape[1])] + [_full(w.shape) for w in weights],
        out_specs=rows(d), out_shape=jax.ShapeDtypeStruct((r, d), F32),
        compiler_params=_params("parallel"), name="mlp",
    )(x, o_cross, *weights)


def _memkv_kernel(mem_ref, g_ref, wk_ref, wv_ref, mk_ref, mv_ref):
    hm = _rmsnorm(mem_ref[0], g_ref[...]).astype(BF16)
    mk_ref[0] = _dot(hm, wk_ref[...])
    mv_ref[0] = _dot(hm, wv_ref[...])


def _memkv(mem, g, w_k, w_v):
    n, m, d = mem.shape
    w = w_k.shape[1]
    blk = lambda width: pl.BlockSpec((1, m, width), lambda b: (b, 0, 0))
    return pl.pallas_call(
        _memkv_kernel, grid=(n,),
        in_specs=[blk(d), _full((1, d)), _full(w_k.shape), _full(w_v.shape)],
        out_specs=[blk(w), blk(w)], out_shape=[jax.ShapeDtypeStruct((n, m, w), F32)] * 2,
        compiler_params=_params("parallel"), name="memkv",
    )(mem, g.reshape(1, d), w_k, w_v)


PROMPT_ROW_TILE = 256
PROMPT_Q_TILE = 128
KEY_GROUP = 4
SSM_TIME_TILE = 256


def kernel(x_prompt, x_sample, mem_prompt, cache_k, cache_v, cache_idx_k, state_ssm_re, state_ssm_im,
           cache_mem_k, cache_mem_v, page_table, rel_bias, norm_mix, w_in, ssm_a_re, ssm_a_im,
           ssm_b_re, ssm_b_im, ssm_c_re, ssm_c_im, ssm_d, ssm_log_dt, w_glu, w_up_attn, w_up_ssm, w_out,
           norm_cross, norm_mem, w_xq, w_xk, w_xv, w_xo, norm_mlp, w_mlp_up, w_mlp_down, norm_final):
    depth = w_in.shape[0]
    nb, seq, d = x_prompt.shape
    ns, t_new, _ = x_sample.shape
    groups, state = ssm_a_re.shape[1], ssm_a_re.shape[2]
    n_mem = mem_prompt.shape[1]
    tm = _row_tile(seq, PROMPT_ROW_TILE)
    tq = _row_tile(seq, PROMPT_Q_TILE)
    ts = _row_tile(seq, SSM_TIME_TILE)
    bf = lambda a: a.astype(BF16)

    bias_prompt, bias_last, bias_new = _bias_tables(rel_bias, tq, t_new)
    yp = x_prompt.reshape(nb * seq, d)
    ys = x_sample.reshape(ns * t_new, d)
    zero_state = jnp.zeros((nb, groups * state), F32)
    outs = {name: [] for name in ("kp", "vp", "ikp", "srp", "sip", "mkp", "mvp", "ks", "vs", "iks", "srs", "sis")}
    pool_kT = cache_k.transpose(0, 1, 3, 4, 2)
    pool_vT = cache_v.transpose(0, 1, 3, 4, 2)
    pool_ikT = cache_idx_k.transpose(0, 1, 3, 2)
    mem_k = cache_mem_k.reshape(depth, ns, n_mem, -1)
    mem_v = cache_mem_v.reshape(depth, ns, n_mem, -1)
    feature_major = lambda a, heads: a.reshape(nb, heads, -1, seq).transpose(0, 3, 1, 2)

    for l in range(depth):
        ws = _split_w_in(w_in[l])
        ssm_params = _ssm_params(ssm_a_re[l], ssm_a_im[l], ssm_b_re[l], ssm_b_im[l], ssm_c_re[l], ssm_c_im[l],
                                 ssm_log_dt[l])
        dense = (bf(w_up_attn[l]), bf(w_up_ssm[l]), bf(w_out[l]), norm_cross[l], bf(w_xq[l]))
        mlp_w = (bf(w_xo[l]), norm_mlp[l], bf(w_mlp_up[l]), bf(w_mlp_down[l]), norm_final)
        glu = bf(w_glu[l])

        qT, iqw, iwT, kT, vT, ikT, kb, vTb, ikb, u, gates = _proj_prompt(yp.reshape(nb, seq, d), norm_mix[l], ws, tm,
                                                                         tq)
        attn = _dsa_prompt(qT, iqw, iwT, ikb, kb, vTb, bias_prompt, tq)
        ssm, s_re, s_im = _ssm(u, zero_state, zero_state, ssm_params, ssm_d[l], glu, ts)
        yp, qc = _merge(yp, attn.reshape(nb * seq, -1), ssm.reshape(nb * seq, -1), gates.reshape(nb * seq, -1),
                        *dense, tm)
        mk, mv = _memkv(mem_prompt, norm_mem[l], bf(w_xk[l]), bf(w_xv[l]))
        oc = _cross(qc.reshape(nb, seq, -1), mk[None], mv[None], 0, tm)
        yp = _mlp(yp, oc.reshape(nb * seq, -1), *mlp_w, tm, l == depth - 1)
        outs["kp"].append(feature_major(kT, N_HEADS))
        outs["vp"].append(feature_major(vT, N_HEADS))
        outs["ikp"].append(ikT.transpose(0, 2, 1))
        outs["srp"].append(s_re.reshape(nb, groups, state))
        outs["sip"].append(s_im.reshape(nb, groups, state))
        outs["mkp"].append(mk.reshape(nb, n_mem, X_HEADS, X_HEAD_DIM))
        outs["mvp"].append(mv.reshape(nb, n_mem, X_HEADS, X_HEAD_DIM))

        q, iq, iw, k, v, ik, u, gates = _proj_rows(ys, norm_mix[l], ws)
        r3 = lambda a: a.reshape(ns, t_new, -1)
        attn = _dsa_sample(r3(q), r3(iq), r3(iw), r3(k), r3(v), r3(ik), pool_kT, pool_vT, pool_ikT, l,
                           page_table, bias_last, bias_new)
        ssm, s_re, s_im = _ssm(r3(u), state_ssm_re[l].reshape(ns, -1), state_ssm_im[l].reshape(ns, -1),
                               ssm_params, ssm_d[l], glu, t_new)
        ys, qc = _merge(ys, attn.reshape(ns * t_new, -1), ssm.reshape(ns * t_new, -1), gates, *dense, ns * t_new)
        oc = _cross(r3(qc), mem_k, mem_v, l, t_new)
        ys = _mlp(ys, oc.reshape(ns * t_new, -1), *mlp_w, ns * t_new, l == depth - 1)
        outs["ks"].append(k.reshape(ns, t_new, N_HEADS, HEAD_DIM))
        outs["vs"].append(v.reshape(ns, t_new, N_HEADS, HEAD_DIM))
        outs["iks"].append(ik.reshape(ns, t_new, IDX_DIM))
        outs["srs"].append(s_re.reshape(ns, groups, state))
        outs["sis"].append(s_im.reshape(ns, groups, state))

    st = lambda name: jnp.stack(outs[name])
    return (yp.reshape(nb, seq, d), ys.reshape(ns, t_new, d),
            st("kp"), st("vp"), st("ikp"), st("srp"), st("sip"), st("mkp"), st("mvp"),
            st("ks"), st("vs"), st("iks"), st("srs"), st("sis"))
```

```python
import functools
import math

import jax
import jax.numpy as jnp
import numpy as np
from jax import lax
from jax.experimental import pallas as pl
from jax.experimental.pallas import tpu as pltpu

F32 = jnp.float32
BF16 = jnp.bfloat16
I32 = jnp.int32

EPS = 1e-6
N_HEADS = 8
HEAD_DIM = 64
ATTN_WIDTH = N_HEADS * HEAD_DIM
IDX_HEADS = 8
IDX_DIM = 64
TOP_K_MAX = 256
SSM_GROUP = 16
SSM_STATE = 64
X_HEADS = 4
X_HEAD_DIM = 128
N_BUCKETS = 32
MAX_DISTANCE = 128
PAGE_SIZE = 128
V_ROWS = HEAD_DIM + 16

MXU_DEPTH = 256
LANES = 128
SUBLANES = 8
VMEM_LIMIT_BYTES = 56 * 1024 * 1024
INT_MIN = -(2 ** 31)
INT_MAX = 2 ** 31 - 1
HIGH_HALF = -(2 ** 16)
HIGH_BIT16 = 2 ** 15
BF16_MIN_NORMAL_BITS = 2 ** 7
F32_TINY = float(np.finfo(np.float32).tiny)
F32_MIN = float(np.finfo(np.float32).min)
F32_MAX = float(np.finfo(np.float32).max)
M_INIT = -1e30

NT_DIMS = (((1,), (1,)), ((), ()))


def _params(*semantics):
    return pltpu.CompilerParams(dimension_semantics=semantics, vmem_limit_bytes=VMEM_LIMIT_BYTES)


def _rmsnorm(x, g):
    ms = jnp.mean(x * x, axis=-1, keepdims=True)
    return x * lax.rsqrt(ms + EPS) * g


def _dot(a, b):
    return jnp.dot(a, b, preferred_element_type=F32)


def _dot_nt(a, b):
    return lax.dot_general(a, b, NT_DIMS, preferred_element_type=F32)


def _monotone_key(x):
    b = lax.bitcast_convert_type(x, I32)
    key = jnp.where(b < 0, b ^ INT_MAX, b)
    return jnp.where(x == 0.0, 0, key)


def _row_tile(rows, want):
    t = min(rows, want)
    assert rows % t == 0, (rows, t)
    return t


def _t5_bucket(dist):
    max_exact = N_BUCKETS // 2
    d = jnp.maximum(dist, 0)
    df = jnp.maximum(d, 1).astype(F32)
    large = max_exact + (jnp.log(df / max_exact) / math.log(MAX_DISTANCE / max_exact)
                         * (N_BUCKETS - max_exact)).astype(I32)
    large = jnp.minimum(large, N_BUCKETS - 1)
    return jnp.where(d < max_exact, d, large)


def _bias_lookup(rel_ref, bucket, h):
    out = jnp.zeros(bucket.shape, F32)
    for b in range(N_BUCKETS):
        out = jnp.where(bucket == b, rel_ref[b, h], out)
    return out - rel_ref[N_BUCKETS - 1, h]


def _bias_prompt_kernel(rel_ref, o_ref, *, tq):
    r = lax.broadcasted_iota(I32, (2 * tq, tq), 0)
    j = lax.broadcasted_iota(I32, (2 * tq, tq), 1)
    bucket = _t5_bucket(j - r + tq)
    for h in range(N_HEADS):
        o_ref[h] = _bias_lookup(rel_ref, bucket, h)


def _bias_sample_kernel(rel_ref, last_ref, new_ref, *, t_new):
    row = lax.broadcasted_iota(I32, (N_HEADS * t_new, PAGE_SIZE), 0)
    lane = lax.broadcasted_iota(I32, (N_HEADS * t_new, PAGE_SIZE), 1)
    last = jnp.zeros(row.shape, F32)
    new = jnp.zeros(row.shape, F32)
    for h in range(N_HEADS):
        t = row - h * t_new
        in_head = (t >= 0) & (t < t_new)
        last = jnp.where(in_head, _bias_lookup(rel_ref, _t5_bucket(PAGE_SIZE + t - lane), h), last)
        new = jnp.where(in_head, _bias_lookup(rel_ref, _t5_bucket(t - lane), h), new)
    last_ref[...] = last
    new_ref[...] = new


def _bias_tables(rel_bias, tq, t_new):
    smem = pl.BlockSpec(memory_space=pltpu.SMEM)
    prompt = pl.pallas_call(
        functools.partial(_bias_prompt_kernel, tq=tq),
        out_shape=jax.ShapeDtypeStruct((N_HEADS, 2 * tq, tq), F32),
        in_specs=[smem], name="bias_prompt")(rel_bias)
    last, new = pl.pallas_call(
        functools.partial(_bias_sample_kernel, t_new=t_new),
        out_shape=[jax.ShapeDtypeStruct((N_HEADS * t_new, PAGE_SIZE), F32)] * 2,
        in_specs=[smem], name="bias_sample")(rel_bias)
    return prompt, last, new


def _proj_prompt_kernel(x_ref, g_ref, wqT_ref, wiqT_ref, wiwT_ref, wk_ref, wkT_ref, wvT_ref, wik_ref, wikT_ref,
                        wu_ref, wg_ref,
                        qT_ref, iqw_ref, iwT_ref, kT_ref, vT_ref, ikT_ref, kb_ref, vTb_ref, ikb_ref, u_ref, gate_ref,
                        *, tq):
    h = _rmsnorm(x_ref[0], g_ref[...]).astype(BF16)
    qT_ref[0] = _dot_nt(wqT_ref[...], h).astype(BF16)
    iqT = _dot_nt(wiqT_ref[...], h).astype(BF16)
    for blk in range(iqw_ref.shape[1]):
        for hh in range(IDX_HEADS):
            iqw_ref[0, blk, :, hh * tq:(hh + 1) * tq] = iqT[hh * IDX_DIM:(hh + 1) * IDX_DIM, blk * tq:(blk + 1) * tq]
    iwT_ref[0] = _dot_nt(wiwT_ref[...], h) * (IDX_HEADS ** -0.5)
    kT_ref[0] = _dot_nt(wkT_ref[...], h)
    kb_ref[0] = _dot(h, wk_ref[...]).astype(BF16)
    vT = _dot_nt(wvT_ref[...], h)
    vT_ref[0] = vT
    ones = jnp.ones((V_ROWS - HEAD_DIM, vT.shape[1]), BF16)
    for hh in range(N_HEADS):
        vTb_ref[0, hh * V_ROWS:hh * V_ROWS + HEAD_DIM, :] = vT[hh * HEAD_DIM:(hh + 1) * HEAD_DIM, :].astype(BF16)
        vTb_ref[0, hh * V_ROWS + HEAD_DIM:(hh + 1) * V_ROWS, :] = ones
    ikT_ref[0] = _dot_nt(wikT_ref[...], h)
    ikb_ref[0] = _dot(h, wik_ref[...]).astype(BF16)
    u_ref[0] = _dot(h, wu_ref[...])
    gate_ref[0] = jax.nn.sigmoid(_dot(h, wg_ref[...]))


def _proj_rows_kernel(x_ref, g_ref, wq_ref, wiq_ref, wiw_ref, wk_ref, wv_ref, wik_ref, wu_ref, wg_ref,
                      q_ref, iq_ref, iw_ref, k_ref, v_ref, ik_ref, u_ref, gate_ref):
    h = _rmsnorm(x_ref[...], g_ref[...]).astype(BF16)
    q_ref[...] = _dot(h, wq_ref[...])
    iq_ref[...] = _dot(h, wiq_ref[...])
    iw_ref[...] = _dot(h, wiw_ref[...]) * (IDX_HEADS ** -0.5)
    k_ref[...] = _dot(h, wk_ref[...])
    v_ref[...] = _dot(h, wv_ref[...])
    ik_ref[...] = _dot(h, wik_ref[...])
    u_ref[...] = _dot(h, wu_ref[...])
    gate_ref[...] = jax.nn.sigmoid(_dot(h, wg_ref[...]))


def _split_w_in(w_in):
    sizes = (ATTN_WIDTH, ATTN_WIDTH, ATTN_WIDTH, IDX_HEADS * IDX_DIM, IDX_DIM, IDX_HEADS)
    offs = np.cumsum((0,) + sizes)
    d_model = w_in.shape[0]
    ssm_width = (w_in.shape[1] - offs[-1] - 2 * d_model)
    wq, wk, wv, wiq, wik, wiw = (w_in[:, offs[i]:offs[i + 1]] for i in range(6))
    wu = w_in[:, offs[-1]:offs[-1] + ssm_width]
    wg = w_in[:, offs[-1] + ssm_width:]
    wq = wq * (HEAD_DIM ** -0.5)
    wiq = wiq * (IDX_DIM ** -0.5)
    return tuple(w.astype(BF16) for w in (wq, wk, wv, wiq, wik, wiw, wu, wg))


def _full(shape):
    return pl.BlockSpec(shape, lambda *_: (0,) * len(shape))


def _proj_prompt(x, g, ws, tm, tq):
    n, t, d = x.shape
    wq, wk, wv, wiq, wik, wiw, wu, wg = ws
    sw, gw = wu.shape[1], wg.shape[1]
    assert tm % tq == 0
    weights = (wq.T, wiq.T, wiw.T, wk, wk.T, wv.T, wik, wik.T, wu, wg)
    rows = lambda w: pl.BlockSpec((1, tm, w), lambda b, i: (b, i, 0))
    cols = lambda w: pl.BlockSpec((1, w, tm), lambda b, i: (b, 0, i))
    iq_wide = IDX_HEADS * tq
    out_shape = [
        jax.ShapeDtypeStruct((n, ATTN_WIDTH, t), BF16),
        jax.ShapeDtypeStruct((n, t // tq, IDX_DIM, iq_wide), BF16),
        jax.ShapeDtypeStruct((n, IDX_HEADS, t), F32),
        jax.ShapeDtypeStruct((n, ATTN_WIDTH, t), F32),
        jax.ShapeDtypeStruct((n, ATTN_WIDTH, t), F32),
        jax.ShapeDtypeStruct((n, IDX_DIM, t), F32),
        jax.ShapeDtypeStruct((n, t, ATTN_WIDTH), BF16),
        jax.ShapeDtypeStruct((n, N_HEADS * V_ROWS, t), BF16),
        jax.ShapeDtypeStruct((n, t, IDX_DIM), BF16),
        jax.ShapeDtypeStruct((n, t, sw), F32),
        jax.ShapeDtypeStruct((n, t, gw), F32),
    ]
    out_specs = [cols(ATTN_WIDTH), pl.BlockSpec((1, tm // tq, IDX_DIM, iq_wide), lambda b, i: (b, i, 0, 0)),
                 cols(IDX_HEADS), cols(ATTN_WIDTH), cols(ATTN_WIDTH), cols(IDX_DIM), rows(ATTN_WIDTH),
                 cols(N_HEADS * V_ROWS), rows(IDX_DIM), rows(sw), rows(gw)]
    return pl.pallas_call(
        functools.partial(_proj_prompt_kernel, tq=tq), grid=(n, t // tm), out_shape=out_shape,
        in_specs=[rows(d), _full((1, d))] + [_full(w.shape) for w in weights],
        out_specs=out_specs, compiler_params=_params("parallel", "parallel"), name="proj_prompt",
    )(x, g.reshape(1, d), *weights)


def _proj_rows(x, g, ws):
    r, d = x.shape
    wq, wk, wv, wiq, wik, wiw, wu, wg = ws
    weights = (wq, wiq, wiw, wk, wv, wik, wu, wg)
    out_shape = [jax.ShapeDtypeStruct((r, w.shape[1]), F32) for w in weights]
    return pl.pallas_call(
        _proj_rows_kernel, out_shape=out_shape, compiler_params=_params(), name="proj_rows",
    )(x, g.reshape(1, d), *weights)


def _dsa_prompt_kernel(qT_ref, iqw_ref, iwT_ref, ikb_ref, kb_ref, vT_ref, bias_ref, o_ref,
                       s_ref, h_ref, pen_ref, pnear_ref, q2_ref, sbuf_ref, pbuf_ref, acc_ref, m_ref, l_ref, j_ref,
                       *, tq, top_k, idx_bits):
    i = pl.program_id(1)
    gk = KEY_GROUP * tq
    n_grp = lax.div(i + KEY_GROUP, KEY_GROUP)
    n_far_grp = lax.div(jnp.maximum(i - 1, 0) + KEY_GROUP - 1, KEY_GROUP)
    iw = iwT_ref[0]
    row = lax.broadcasted_iota(I32, (tq, tq), 0)
    col = lax.broadcasted_iota(I32, (tq, tq), 1)
    grow = lax.broadcasted_iota(I32, (gk, tq), 0)

    ahead = row - col

    def score_group(g, has_future):
        for cc in range(KEY_GROUP):
            c = g * KEY_GROUP + cc
            r0 = pl.multiple_of(c * tq, tq)
            d = _dot(ikb_ref[0, pl.ds(r0, tq), :], iqw_ref[0, 0])
            sc = jnp.zeros((tq, tq), F32)
            for h in range(IDX_HEADS):
                sc = sc + iw[h:h + 1, :] * jnp.maximum(d[:, h * tq:(h + 1) * tq], 0.0)
            bits = lax.bitcast_convert_type(sc, I32)
            tiny = jnp.abs(sc) < F32_TINY
            key = jnp.where(tiny, 0, jnp.where(bits < 0, bits ^ INT_MAX, bits))
            top = jnp.where(tiny, 0.0, lax.bitcast_convert_type(bits & HIGH_HALF, F32))
            if has_future:
                future = ahead > (i - c) * tq
                key = jnp.where(future, INT_MIN, key)
                top = jnp.where(future, -jnp.inf, top)
            s_ref[pl.ds(r0, tq), :] = key
            h_ref[pl.ds(r0, tq), :] = top.astype(BF16)

    def past_group(g, carry):
        score_group(g, False)
        return carry

    lax.fori_loop(0, n_grp - 1, past_group, 0)
    score_group(n_grp - 1, True)

    one_b = jnp.ones((gk, tq), BF16)
    zero_b = jnp.zeros((gk, tq), BF16)
    pack = 2 * SUBLANES

    def count_top(cand):
        def body(g, acc):
            r0 = pl.multiple_of(g * gk, gk)
            ind = jnp.where(h_ref[pl.ds(r0, gk), :] >= cand, one_b, zero_b)
            parts = [ind[k * pack:(k + 1) * pack, :] for k in range(gk // pack)]
            while len(parts) > 1:
                parts = [a + b for a, b in zip(parts[0::2], parts[1::2])]
            return acc + parts[0].astype(F32)
        acc = lax.fori_loop(0, n_grp, body, jnp.zeros((pack, tq), F32))
        return acc.sum(axis=0, keepdims=True).astype(I32)

    def count(pred):
        def body(g, acc8):
            r0 = pl.multiple_of(g * gk, gk)
            ind = jnp.where(pred(s_ref[pl.ds(r0, gk), :], r0 + grow), 1, 0)
            return acc8 + ind.reshape(gk // SUBLANES, SUBLANES, tq).sum(axis=0)
        acc8 = lax.fori_loop(0, n_grp, body, jnp.zeros((SUBLANES, tq), I32))
        return acc8.sum(axis=0, keepdims=True)

    def accept(cand_u, cnt, state):
        tu, cnt_ge = state
        ok = cnt >= top_k
        return jnp.where(ok, cand_u, tu), jnp.where(ok, cnt, cnt_ge)

    def top_step(it, state):
        cand_u = state[0] | lax.shift_left(jnp.int32(1), 31 - it)
        key16 = lax.shift_right_arithmetic(cand_u ^ INT_MIN, 16)
        key16 = jnp.where((key16 > 0) & (key16 < BF16_MIN_NORMAL_BITS), BF16_MIN_NORMAL_BITS, key16)
        key16 = jnp.where((key16 < 0) & (key16 >= -BF16_MIN_NORMAL_BITS), 0, key16)
        bits = lax.shift_left(jnp.where(key16 < 0, key16 ^ (HIGH_BIT16 - 1), key16), 16)
        cand = lax.bitcast_convert_type(bits, F32).astype(BF16)
        return accept(cand_u, count_top(cand), state)

    def low_step(it, state):
        cand_u = state[0] | lax.shift_left(jnp.int32(1), 31 - it)
        cand = cand_u ^ INT_MIN
        return accept(cand_u, count(lambda blk, r0: blk >= cand), state)

    state = (jnp.zeros((1, tq), I32), jnp.zeros((1, tq), I32))
    state = lax.fori_loop(0, 16, top_step, state)
    tu, cnt_ge = lax.fori_loop(16, 32, low_step, state)
    thr = jnp.maximum(tu ^ INT_MIN, INT_MIN + 1)
    multi = cnt_ge > top_k

    j_ref[...] = jnp.full(j_ref.shape, INT_MAX, I32)

    @pl.when(jnp.max(multi.astype(I32)) > 0)
    def _():
        need = top_k - count(lambda blk, r0: blk > thr)

        def j_step(it, jv):
            cand = jv | lax.shift_left(jnp.int32(1), idx_bits - 1 - it)
            cnt = count(lambda blk, ridx: (blk == thr) & (ridx < cand))
            return jnp.where(cnt < need, cand, jv)
        jv = lax.fori_loop(0, idx_bits, j_step, jnp.zeros((1, tq), I32))
        j_ref[0:1, :] = jnp.where(multi, jv, INT_MAX)

    jsel = j_ref[0:1, :]

    def selected(blk, ridx):
        return (blk > thr) | ((blk == thr) & (ridx <= jsel))

    def pen_group(g, carry):
        r0 = pl.multiple_of(g * gk, gk)
        ridx = r0 + grow
        sel = selected(s_ref[pl.ds(r0, gk), :], ridx) & (ridx < (i - 1) * tq)
        pen_ref[pl.ds(r0, gk), :] = jnp.where(sel, 0.0, F32_MIN)
        return carry

    lax.fori_loop(0, jnp.maximum(n_far_grp, 1), pen_group, 0)
    w0 =pl.multiple_of(jnp.maximum(i - 1, 0) * tq, tq)
    pnear_ref[...] = jnp.where(selected(s_ref[pl.ds(w0, 2 * tq), :], w0 + grow[0:2 * tq]), 0.0, F32_MIN)

    zero = jnp.zeros((HEAD_DIM, tq), BF16)
    for hp in range(N_HEADS // 2):
        lo = qT_ref[0, (2 * hp) * HEAD_DIM:(2 * hp + 1) * HEAD_DIM, :]
        hi = qT_ref[0, (2 * hp + 1) * HEAD_DIM:(2 * hp + 2) * HEAD_DIM, :]
        q2_ref[hp * 2 * HEAD_DIM:(hp + 1) * 2 * HEAD_DIM, :] = jnp.concatenate(
            [jnp.concatenate([lo, zero], axis=1), jnp.concatenate([zero, hi], axis=1)], axis=0)
    m_ref[...] = jnp.full(m_ref.shape, M_INIT, F32)
    l_ref[...] = jnp.zeros(l_ref.shape, F32)
    acc_ref[...] = jnp.zeros(acc_ref.shape, F32)

    def update_head(h, m8, r0, n_c):
        hs = slice(h * HEAD_DIM, (h + 1) * HEAD_DIM)
        ls = slice(h * SUBLANES, (h + 1) * SUBLANES)
        m_old = m_ref[h:h + 1, :]
        m_new = jnp.maximum(m_old, m8.max(axis=0, keepdims=True))
        alpha = jnp.exp(m_old - m_new)
        for cc in range(n_c):
            p = jnp.exp(sbuf_ref[h, cc * tq:(cc + 1) * tq, :] - m_new)
            pbuf_ref[h, cc * tq:(cc + 1) * tq, :] = p.astype(BF16)
        pv = _dot(vT_ref[0, h * V_ROWS:(h + 1) * V_ROWS, pl.ds(r0, n_c * tq)], pbuf_ref[h, 0:n_c * tq, :])
        acc_ref[hs, :] = alpha * acc_ref[hs, :] + pv[0:HEAD_DIM, :]
        l_ref[ls, :] = alpha * l_ref[ls, :] + pv[HEAD_DIM:HEAD_DIM + SUBLANES, :]
        m_ref[h:h + 1, :] = m_new

    def logits(hp, r0, n_c, pen_src, pen_r0, bias_rows):
        m8 = [None, None]
        for cc in range(n_c):
            pen = pen_src[pl.ds(pen_r0 + cc * tq, tq), :]
            kslab = kb_ref[0, pl.ds(r0 + cc * tq, tq), hp * 2 * HEAD_DIM:(hp + 1) * 2 * HEAD_DIM]
            s2 = _dot(kslab, q2_ref[hp * 2 * HEAD_DIM:(hp + 1) * 2 * HEAD_DIM, :])
            for e in range(2):
                h = 2 * hp + e
                s = s2[:, e * tq:(e + 1) * tq] + pen
                if bias_rows is not None:
                    s = s + bias_ref[h, pl.ds(bias_rows[cc], tq), :]
                sbuf_ref[h, cc * tq:(cc + 1) * tq, :] = s
                cm = s.reshape(tq // SUBLANES, SUBLANES, tq).max(axis=0)
                m8[e] = cm if m8[e] is None else jnp.maximum(m8[e], cm)
        return tuple(m8)

    def attend_group(m8, r0, n_c, pen_src, pen_r0, bias_rows, next_group_logits):
        pairs = N_HEADS // 2
        for hp in range(pairs):
            if hp + 1 < pairs:
                m8_next = logits(hp + 1, r0, n_c, pen_src, pen_r0, bias_rows)
            else:
                m8_next = next_group_logits() if next_group_logits is not None else None
            for e in range(2):
                update_head(2 * hp + e, m8[e], r0, n_c)
            m8 = m8_next
        return m8

    def far_logits0(g):
        r0 = pl.multiple_of(g * gk, gk)
        return logits(0, r0, KEY_GROUP, pen_ref, r0, None)

    def far_group(g, m8):
        r0 = pl.multiple_of(g * gk, gk)
        nxt = jnp.minimum(g + 1, n_far_grp - 1)
        return attend_group(m8, r0, KEY_GROUP, pen_ref, r0, None, lambda: far_logits0(nxt))

    lax.fori_loop(0, n_far_grp, far_group, far_logits0(0))
    first_bias = pl.multiple_of(jnp.where(i == 0, tq, 0), tq)
    near = (w0, 2, pnear_ref, 0, (first_bias, tq))
    attend_group(logits(0, *near), *near, None)

    for h in range(N_HEADS):
        hs = slice(h * HEAD_DIM, (h + 1) * HEAD_DIM)
        acc_ref[hs, :] = acc_ref[hs, :] / l_ref[h * SUBLANES:h * SUBLANES + 1, :]
    o_ref[0] = acc_ref[...].T.astype(BF16)


def _dsa_prompt(qT, iqw, iwT, ikb, kb, vT, bias, tq):
    n, _, t = qT.shape
    n_blk = t // tq
    assert t % tq == 0 and tq >= MAX_DISTANCE and n_blk % KEY_GROUP == 0 and n_blk >= 2, (t, tq)
    top_k = min(TOP_K_MAX, t // 4)
    idx_bits = max(1, int(math.ceil(math.log2(t))))
    gk = KEY_GROUP * tq
    colblk = lambda w: pl.BlockSpec((1, w, tq), lambda b, i: (b, 0, i))
    whole = lambda a: pl.BlockSpec((1,) + a.shape[1:], lambda b, i: (b, 0, 0), pipeline_mode=pl.Buffered(1))
    kern = functools.partial(_dsa_prompt_kernel, tq=tq, top_k=top_k, idx_bits=idx_bits)
    return pl.pallas_call(
        kern, grid=(n, n_blk),
        out_shape=jax.ShapeDtypeStruct((n, t, ATTN_WIDTH), BF16),
        in_specs=[colblk(ATTN_WIDTH), pl.BlockSpec((1, 1) + iqw.shape[2:], lambda b, i: (b, i, 0, 0)),
                  colblk(IDX_HEADS), whole(ikb), whole(kb), whole(vT),
                  pl.BlockSpec(bias.shape, lambda b, i: (0, 0, 0), pipeline_mode=pl.Buffered(1))],
        out_specs=pl.BlockSpec((1, tq, ATTN_WIDTH), lambda b, i: (b, i, 0)),
        scratch_shapes=[pltpu.VMEM((t, tq), I32),
                        pltpu.VMEM((t, tq), BF16),
                        pltpu.VMEM((t, tq), F32),
                        pltpu.VMEM((2 * tq, tq), F32),
                        pltpu.VMEM((N_HEADS * HEAD_DIM, 2 * tq), BF16),
                        pltpu.VMEM((N_HEADS, gk, tq), F32),
                        pltpu.VMEM((N_HEADS, gk, tq), BF16),
                        pltpu.VMEM((ATTN_WIDTH, tq), F32),
                        pltpu.VMEM((N_HEADS, tq), F32),
                        pltpu.VMEM((N_HEADS * SUBLANES, tq), F32),
                        pltpu.VMEM((SUBLANES, tq), I32)],
        compiler_params=_params("parallel", "arbitrary"), name="dsa_prompt",
    )(qT, iqw, iwT, ikb, kb, vT, bias)


def _page_specs(page_shape, layer, pages_per_step):
    def spec(r):
        return pl.BlockSpec((1, 1) + page_shape,
                            lambda b, g, pt: (layer, pt[b, g * pages_per_step + r]) + (0,) * len(page_shape))
    return [spec(r) for r in range(pages_per_step)]


def _head_sum(x, t_new):
    return x.reshape(IDX_HEADS, t_new, x.shape[-1]).sum(axis=0)


def _dsa_sample_score_kernel(pt_ref, *refs, pps, t_new):
    page_refs = refs[:pps]
    iq_ref, w_ref, iknew_ref, sp_ref, sn_ref = refs[pps:]
    g = pl.program_id(1)
    iq = iq_ref[0]
    w = w_ref[0]
    for r in range(pps):
        d = _dot(iq, page_refs[r][0, 0].astype(BF16))
        sc = _head_sum(w * jnp.maximum(d, 0.0), t_new)
        sp_ref[0, :, r * PAGE_SIZE:(r + 1) * PAGE_SIZE] = _monotone_key(sc)

    @pl.when(g == pl.num_programs(1) - 1)
    def _():
        d = _dot_nt(iq, iknew_ref[0].astype(BF16))
        sc = _head_sum(w * jnp.maximum(d, 0.0), t_new)
        t = lax.broadcasted_iota(I32, sc.shape, 0)
        j = lax.broadcasted_iota(I32, sc.shape, 1)
        sn_ref[0] = jnp.where(j > t, INT_MIN, _monotone_key(sc))


def _dsa_sample_thr_kernel(sp_ref, sn_ref, thr_ref, j_ref, *, top_k, idx_bits):
    rows, past = sp_ref.shape
    lane = lax.broadcasted_iota(I32, (rows, LANES), 1)

    def count(pred):
        tot = jnp.where(pred(sn_ref[...], past + lane), 1, 0)
        for c in range(past // LANES):
            tot = tot + jnp.where(pred(sp_ref[:, c * LANES:(c + 1) * LANES], c * LANES + lane), 1, 0)
        return jnp.broadcast_to(tot.sum(axis=-1, keepdims=True), (rows, LANES))

    def bit_step(it, tu):
        cand_u = tu | lax.shift_left(jnp.int32(1), 31 - it)
        cand = cand_u ^ INT_MIN
        cnt = count(lambda blk, idx: blk >= cand)
        return jnp.where(cnt >= top_k, cand_u, tu)

    tu = lax.fori_loop(0, 32, bit_step, jnp.zeros((rows, LANES), I32))
    thr = jnp.maximum(tu ^ INT_MIN, INT_MIN + 1)
    cnt_gt = count(lambda blk, idx: blk > thr)
    cnt_ge = count(lambda blk, idx: blk >= thr)
    need = top_k - cnt_gt
    multi = cnt_ge > top_k
    thr_ref[...] = thr
    j_ref[...] = jnp.full((rows, LANES), INT_MAX, I32)

    @pl.when(jnp.max(multi.astype(I32)) > 0)
    def _():
        def j_step(it, jv):
            cand = jv | lax.shift_left(jnp.int32(1), idx_bits - 1 - it)
            cnt = count(lambda blk, idx: (blk == thr) & (idx < cand))
            return jnp.where(cnt < need, cand, jv)
        jv = lax.fori_loop(0, idx_bits, j_step, jnp.zeros((rows, LANES), I32))
        j_ref[...] = jnp.where(multi, jv, INT_MAX)


def _dsa_sample_attn_kernel(pt_ref, *refs, pps, t_new):
    k_refs = refs[:pps]
    v_refs = refs[pps:2 * pps]
    (q_ref, sp_ref, sn_ref, thr_ref, j_ref, knew_ref, vnew_ref, blast_ref, bnew_ref,
     o_ref, acc_ref, m_ref, l_ref) = refs[2 * pps:]
    g = pl.program_id(1)
    last = g == pl.num_programs(1) - 1
    thr = thr_ref[0]
    jsel = j_ref[0]
    lane = lax.broadcasted_iota(I32, (t_new, PAGE_SIZE), 1)

    @pl.when(g == 0)
    def _():
        m_ref[...] = jnp.full(m_ref.shape, M_INIT, F32)
        l_ref[...] = jnp.zeros(l_ref.shape, F32)
        acc_ref[...] = jnp.zeros(acc_ref.shape, F32)

    def update(pages):
        logit = []
        for keys, idx, kT_ref, _, bias in pages:
            s = jnp.concatenate([_dot(q_ref[0, h], kT_ref[h].astype(BF16)) for h in range(N_HEADS)], axis=0)
            if bias is not None:
                s = s + bias
            sel = (keys > thr) | ((keys == thr) & (idx <= jsel))
            cap = jnp.where(sel, F32_MAX, F32_MIN)
            logit.append(jnp.minimum(s, jnp.concatenate([cap] * N_HEADS, axis=0)))
        m_old = m_ref[...]
        m_new = jnp.maximum(m_old, jnp.broadcast_to(functools.reduce(jnp.maximum, logit).max(axis=-1, keepdims=True),
                                                    m_old.shape))
        prob = [jnp.exp(s - m_new) for s in logit]
        alpha = jnp.exp(m_old - m_new)
        l_ref[...] = alpha * l_ref[...] + jnp.broadcast_to(sum(prob).sum(axis=-1, keepdims=True), m_old.shape)
        pv = sum(jnp.concatenate([_dot_nt(p[h * t_new:(h + 1) * t_new, :].astype(BF16), page[3][h].astype(BF16))
                                  for h in range(N_HEADS)], axis=0)
                 for p, page in zip(prob, pages))
        acc_ref[...] = alpha[:, 0:HEAD_DIM] * acc_ref[...] + pv
        m_ref[...] = m_new

    last_f = jnp.where(last, 1.0, 0.0)
    update([(sp_ref[0, :, r * PAGE_SIZE:(r + 1) * PAGE_SIZE], (g * pps + r) * PAGE_SIZE + lane,
             k_refs[r].at[0, 0], v_refs[r].at[0, 0],
             blast_ref[...] * last_f if r == pps - 1 else None)
            for r in range(pps)])

    @pl.when(last)
    def _():
        n_past = pl.num_programs(1) * pps * PAGE_SIZE
        update([(sn_ref[0], n_past + lane, knew_ref.at[0], vnew_ref.at[0], bnew_ref[...])])
        o_full = acc_ref[...] / l_ref[:, 0:HEAD_DIM]
        for h in range(N_HEADS):
            o_ref[0, h] = o_full[h * t_new:(h + 1) * t_new, :].astype(BF16)


def _dsa_sample(q, iq, iw, k_new, v_new, ik_new, pool_kT, pool_vT, pool_ikT, layer, page_table, bias_last,
                bias_new):
    n, t_new, _ = q.shape
    n_pages = page_table.shape[1]
    past = n_pages * PAGE_SIZE
    top_k = min(TOP_K_MAX, (past + t_new) // 4)
    ht = N_HEADS * t_new
    assert t_new <= PAGE_SIZE and IDX_HEADS == N_HEADS

    heads = lambda a, dim: a.reshape(n, t_new, N_HEADS, dim).transpose(0, 2, 1, 3)
    iq_hm = heads(iq, IDX_DIM).reshape(n, ht, IDX_DIM).astype(BF16)
    w_hm = jnp.broadcast_to(iw.transpose(0, 2, 1).reshape(n, ht, 1), (n, ht, LANES))
    q_hm = heads(q, HEAD_DIM).astype(BF16)
    page = lambda a: jnp.pad(heads(a, HEAD_DIM).transpose(0, 1, 3, 2),
                             ((0, 0), (0, 0), (0, 0), (0, PAGE_SIZE - t_new)))
    k_page, v_page = page(k_new), page(v_new)
    ik_pad = jnp.pad(ik_new, ((0, 0), (0, PAGE_SIZE - t_new), (0, 0)))

    per_b = lambda a: pl.BlockSpec((1,) + a.shape[1:], lambda b, g, pt: (b,) + (0,) * (a.ndim - 1))
    const = lambda a: pl.BlockSpec(a.shape, lambda b, g, pt: (0,) * a.ndim)
    sp_spec = lambda pages: pl.BlockSpec((1, t_new, pages * PAGE_SIZE), lambda b, g, pt: (b, 0, g))

    pps = math.gcd(n_pages, SCORE_PAGES_PER_STEP)
    sp, sn = pl.pallas_call(
        functools.partial(_dsa_sample_score_kernel, pps=pps, t_new=t_new),
        grid_spec=pltpu.PrefetchScalarGridSpec(
            num_scalar_prefetch=1, grid=(n, n_pages // pps),
            in_specs=_page_specs((IDX_DIM, PAGE_SIZE), layer, pps) + [per_b(iq_hm), per_b(w_hm), per_b(ik_pad)],
            out_specs=[sp_spec(pps), pl.BlockSpec((1, t_new, PAGE_SIZE), lambda b, g, pt: (b, 0, 0))]),
        out_shape=[jax.ShapeDtypeStruct((n, t_new, past), I32), jax.ShapeDtypeStruct((n, t_new, PAGE_SIZE), I32)],
        compiler_params=_params("parallel", "arbitrary"), name="dsa_sample_score",
    )(page_table, *([pool_ikT] * pps), iq_hm, w_hm, ik_pad)

    rows = n * t_new
    rt = _row_tile(rows, 32)
    idx_bits = int(math.ceil(math.log2(past + PAGE_SIZE)))
    thr, jsel = pl.pallas_call(
        functools.partial(_dsa_sample_thr_kernel, top_k=top_k, idx_bits=idx_bits),
        grid=(rows // rt,),
        in_specs=[pl.BlockSpec((rt, past), lambda i: (i, 0)), pl.BlockSpec((rt, PAGE_SIZE), lambda i: (i, 0))],
        out_specs=[pl.BlockSpec((rt, LANES), lambda i: (i, 0))] * 2,
        out_shape=[jax.ShapeDtypeStruct((rows, LANES), I32)] * 2,
        compiler_params=_params("parallel"), name="dsa_sample_thr",
    )(sp.reshape(rows, past), sn.reshape(rows, PAGE_SIZE))
    thr = thr.reshape(n, t_new, LANES)
    jsel = jsel.reshape(n, t_new, LANES)

    kv_page = (N_HEADS, HEAD_DIM, PAGE_SIZE)
    pps = math.gcd(n_pages, ATTN_PAGES_PER_STEP)
    o = pl.pallas_call(
        functools.partial(_dsa_sample_attn_kernel, pps=pps, t_new=t_new),
        grid_spec=pltpu.PrefetchScalarGridSpec(
            num_scalar_prefetch=1, grid=(n, n_pages // pps),
            in_specs=(_page_specs(kv_page, layer, pps) + _page_specs(kv_page, layer, pps)
                      + [per_b(q_hm), sp_spec(pps), per_b(sn), per_b(thr), per_b(jsel), per_b(k_page), per_b(v_page),
                         const(bias_last), const(bias_new)]),
            out_specs=pl.BlockSpec((1, N_HEADS, t_new, HEAD_DIM), lambda b, g, pt: (b, 0, 0, 0)),
            scratch_shapes=[pltpu.VMEM((ht, HEAD_DIM), F32), pltpu.VMEM((ht, LANES), F32),
                            pltpu.VMEM((ht, LANES), F32)]),
        out_shape=jax.ShapeDtypeStruct((n, N_HEADS, t_new, HEAD_DIM), BF16),
        compiler_params=_params("parallel", "arbitrary"), name="dsa_sample_attn",
    )(page_table, *([pool_kT] * pps), *([pool_vT] * pps), q_hm, sp, sn, thr, jsel, k_page, v_page, bias_last,
      bias_new)
    return o.transpose(0, 2, 1, 3).reshape(n, t_new, ATTN_WIDTH)


def _gelu_tanh(y):
    return 0.5 * y * (1.0 + jnp.tanh(math.sqrt(2.0 / math.pi) * (y + 0.044715 * (y * y * y))))


def _ssm_kernel(u_ref, x0r_ref, x0i_ref, lr_ref, li_ref, b_ref, c_ref, d_ref, wglu_ref,
                y_ref, sr_ref, si_ref, x_ref, st_ref, *, tile):
    j = pl.program_id(1)
    ns = lr_ref.shape[1]

    @pl.when(j == 0)
    def _():
        st_ref[0:1, :] = x0r_ref[0]
        st_ref[1:2, :] = x0i_ref[0]

    u = u_ref[0]
    ub = u.astype(BF16)
    clusters = max(1, u.shape[1] // MXU_DEPTH)
    cw = u.shape[1] // clusters
    sw = ns // clusters
    for k in range(clusters):
        for part in range(2):
            cols = slice(part * ns + k * sw, part * ns + (k + 1) * sw)
            x_ref[:, cols] = _dot(ub[:, k * cw:(k + 1) * cw], b_ref[k * cw:(k + 1) * cw, cols])
    lr = lr_ref[...]
    li = li_ref[...]

    def step(t, carry):
        sr, si = carry
        br = x_ref[pl.ds(t, 1), 0:ns]
        bi = x_ref[pl.ds(t, 1), ns:2 * ns]
        nr = lr * sr - li * si + br
        ni = lr * si + li * sr + bi
        x_ref[pl.ds(t, 1), 0:ns] = nr
        x_ref[pl.ds(t, 1), ns:2 * ns] = ni
        return nr, ni

    sr, si = lax.fori_loop(0, tile, step, (st_ref[0:1, :], st_ref[1:2, :]))
    st_ref[0:1, :] = sr
    st_ref[1:2, :] = si

    cx = []
    for k in range(clusters):
        ch = slice(k * cw, (k + 1) * cw)
        re = slice(k * sw, (k + 1) * sw)
        im = slice(ns + k * sw, ns + (k + 1) * sw)
        cx.append(_dot(x_ref[:, re].astype(BF16), c_ref[re, ch]) + _dot(x_ref[:, im].astype(BF16), c_ref[im, ch]))
    y = jnp.concatenate(cx, axis=1) + d_ref[...] * u
    y = _gelu_tanh(y)
    y_ref[0] = (y * jax.nn.sigmoid(_dot(y.astype(BF16), wglu_ref[...]))).astype(BF16)

    @pl.when(j == pl.num_programs(1) - 1)
    def _():
        sr_ref[0] = sr
        si_ref[0] = si


def _ssm_params(a_re, a_im, b_re, b_im, c_re, c_im, log_dt):
    g = a_re.shape[0]
    dt = jnp.exp(log_dt)[:, None]
    mag = jnp.exp(a_re * dt)
    lam_re, lam_im = mag * jnp.cos(a_im * dt), mag * jnp.sin(a_im * dt)
    den = a_re * a_re + a_im * a_im
    nr, ni = lam_re - 1.0, lam_im
    f_re = (nr * a_re + ni * a_im) / den
    f_im = (ni * a_re - nr * a_im) / den
    bb_re = f_re[..., None] * b_re - f_im[..., None] * b_im
    bb_im = f_re[..., None] * b_im + f_im[..., None] * b_re
    eye = jnp.eye(g, dtype=F32)
    p, c = bb_re.shape[1], bb_re.shape[2]
    blk_in = lambda m: jnp.einsum('gpc,gh->gchp', m, eye).reshape(g * c, g * p)
    blk_out = lambda m: jnp.einsum('gcp,gh->gphc', m, eye).reshape(g * p, g * c)
    b_blk = jnp.concatenate([blk_in(bb_re), blk_in(bb_im)], axis=1).astype(BF16)
    c_blk = jnp.concatenate([blk_out(c_re), -blk_out(c_im)], axis=0).astype(BF16)
    return lam_re.reshape(1, g * p), lam_im.reshape(1, g * p), b_blk, c_blk


def _ssm(u, x0_re, x0_im, params, d_skip, w_glu, tile):
    n, t, w = u.shape
    lam_re, lam_im, b_blk, c_blk = params
    ns = lam_re.shape[1]
    x0_re = x0_re.reshape(n, 1, ns)
    x0_im = x0_im.reshape(n, 1, ns)
    seq = lambda width: pl.BlockSpec((1, tile, width), lambda b, j: (b, j, 0))
    state = pl.BlockSpec((1, 1, ns), lambda b, j: (b, 0, 0))
    y, sr, si = pl.pallas_call(
        functools.partial(_ssm_kernel, tile=tile), grid=(n, t // tile),
        in_specs=[seq(w), state, state, _full((1, ns)), _full((1, ns)), _full(b_blk.shape), _full(c_blk.shape),
                  _full((1, w)), _full(w_glu.shape)],
        out_specs=[seq(w), state, state],
        out_shape=[jax.ShapeDtypeStruct((n, t, w), BF16), jax.ShapeDtypeStruct((n, 1, ns), F32),
                   jax.ShapeDtypeStruct((n, 1, ns), F32)],
        scratch_shapes=[pltpu.VMEM((tile, 2 * ns), F32), pltpu.VMEM((SUBLANES, ns), F32)],
        compiler_params=_params("parallel", "arbitrary"), name="ssm",
    )(u, x0_re, x0_im, lam_re, lam_im, b_blk, c_blk, d_skip.reshape(1, w), w_glu)
    return y, sr, si


def _merge_kernel(x_ref, attn_ref, ssm_ref, gate_ref, wua_ref, wus_ref, wo_ref, gx_ref, wxq_ref, xo_ref, qc_ref):
    d = x_ref.shape[-1]
    gate = gate_ref[...]
    mixed = gate[:, 0:d] * _dot(attn_ref[...], wua_ref[...]) + gate[:, d:2 * d] * _dot(ssm_ref[...], wus_ref[...])
    x = x_ref[...] + _dot(mixed.astype(BF16), wo_ref[...])
    xo_ref[...] = x
    qc_ref[...] = _dot(_rmsnorm(x, gx_ref[...]).astype(BF16), wxq_ref[...]).astype(BF16)


def _merge(x, attn, ssm, gates, w_up_attn, w_up_ssm, w_out, g_cross, w_xq, tm):
    r, d = x.shape
    rows = lambda w: pl.BlockSpec((tm, w), lambda i: (i, 0))
    weights = (w_up_attn, w_up_ssm, w_out, g_cross.reshape(1, d), w_xq)
    return pl.pallas_call(
        _merge_kernel, grid=(r // tm,),
        in_specs=[rows(d), rows(attn.shape[1]), rows(ssm.shape[1]), rows(gates.shape[1])]
        + [_full(w.shape) for w in weights],
        out_specs=[rows(d), rows(w_xq.shape[1])],
        out_shape=[jax.ShapeDtypeStruct((r, d), F32), jax.ShapeDtypeStruct((r, w_xq.shape[1]), BF16)],
        compiler_params=_params("parallel"), name="merge",
    )(x, attn, ssm, gates, *weights)


def _cross_kernel(q_ref, mk_ref, mv_ref, o_ref):
    q = q_ref[0]
    outs = []
    for h in range(X_HEADS):
        sl = slice(h * X_HEAD_DIM, (h + 1) * X_HEAD_DIM)
        s = _dot_nt(q[:, sl], mk_ref[0, 0, :, sl].astype(BF16)) * (X_HEAD_DIM ** -0.5)
        s = s - s.max(axis=-1, keepdims=True)
        p = jnp.exp(s)
        p = (p / p.sum(axis=-1, keepdims=True)).astype(BF16)
        outs.append(_dot(p, mv_ref[0, 0, :, sl].astype(BF16)))
    o_ref[0] = jnp.concatenate(outs, axis=-1).astype(BF16)


def _cross(q, mk, mv, layer, tq):
    n, t, w = q.shape
    mem = pl.BlockSpec((1, 1) + mk.shape[2:], lambda b, i: (layer, b, 0, 0))
    blk = pl.BlockSpec((1, tq, w), lambda b, i: (b, i, 0))
    return pl.pallas_call(
        _cross_kernel, grid=(n, t // tq), in_specs=[blk, mem, mem], out_specs=blk,
        out_shape=jax.ShapeDtypeStruct((n, t, w), BF16),
        compiler_params=_params("parallel", "parallel"), name="cross",
    )(q, mk, mv)


def _mlp_kernel(x_ref, oc_ref, wxo_ref, gm_ref, wup_ref, wdn_ref, gf_ref, o_ref, *, ff_chunk, final):
    x = x_ref[...] + _dot(oc_ref[...], wxo_ref[...])
    h = _rmsnorm(x, gm_ref[...]).astype(BF16)
    acc = jnp.zeros(x.shape, F32)
    for c in range(wup_ref.shape[1] // ff_chunk):
        sl = slice(c * ff_chunk, (c + 1) * ff_chunk)
        a = jnp.maximum(_dot(h, wup_ref[:, sl]), 0.0)
        acc = acc + _dot((a * a).astype(BF16), wdn_ref[sl, :])
    x = x + acc
    o_ref[...] = _rmsnorm(x, gf_ref[...]) if final else x


def _mlp(x, o_cross, w_xo, g_mlp, w_up, w_down, g_final, tm, final):
    r, d = x.shape
    rows = lambda w: pl.BlockSpec((tm, w), lambda i: (i, 0))
    weights = (w_xo, g_mlp.reshape(1, d), w_up, w_down, g_final.reshape(1, d))
    return pl.pallas_call(
        functools.partial(_mlp_kernel, ff_chunk=min(1024, w_up.shape[1]), final=final), grid=(r // tm,),
        in_specs=[rows(d), rows(o_cross.shape[1])] + [_full(w.shape) for w in weights],
        out_specs=rows(d), out_shape=jax.ShapeDtypeStruct((r, d), F32),
        compiler_params=_params("parallel"), name="mlp",
    )(x, o_cross, *weights)


def _memkv_kernel(mem_ref, g_ref, wk_ref, wv_ref, mk_ref, mv_ref):
    hm = _rmsnorm(mem_ref[0], g_ref[...]).astype(BF16)
    mk_ref[0] = _dot(hm, wk_ref[...])
    mv_ref[0] = _dot(hm, wv_ref[...])


def _memkv(mem, g, w_k, w_v):
    n, m, d = mem.shape
    w = w_k.shape[1]
    blk = lambda width: pl.BlockSpec((1, m, width), lambda b: (b, 0, 0))
    return pl.pallas_call(
        _memkv_kernel, grid=(n,),
        in_specs=[blk(d), _full((1, d)), _full(w_k.shape), _full(w_v.shape)],
        out_specs=[blk(w), blk(w)], out_shape=[jax.ShapeDtypeStruct((n, m, w), F32)] * 2,
        compiler_params=_params("parallel"), name="memkv",
    )(mem, g.reshape(1, d), w_k, w_v)


PROMPT_ROW_TILE = 256
PROMPT_Q_TILE = 128
KEY_GROUP = 4
SSM_TIME_TILE = 256
SCORE_PAGES_PER_STEP = 32
ATTN_PAGES_PER_STEP = 16


def kernel(x_prompt, x_sample, mem_prompt, cache_k, cache_v, cache_idx_k, state_ssm_re, state_ssm_im,
           cache_mem_k, cache_mem_v, page_table, rel_bias, norm_mix, w_in, ssm_a_re, ssm_a_im,
           ssm_b_re, ssm_b_im, ssm_c_re, ssm_c_im, ssm_d, ssm_log_dt, w_glu, w_up_attn, w_up_ssm, w_out,
           norm_cross, norm_mem, w_xq, w_xk, w_xv, w_xo, norm_mlp, w_mlp_up, w_mlp_down, norm_final):
    depth = w_in.shape[0]
    nb, seq, d = x_prompt.shape
    ns, t_new, _ = x_sample.shape
    groups, state = ssm_a_re.shape[1], ssm_a_re.shape[2]
    n_mem = mem_prompt.shape[1]
    tm = _row_tile(seq, PROMPT_ROW_TILE)
    tq = _row_tile(seq, PROMPT_Q_TILE)
    ts = _row_tile(seq, SSM_TIME_TILE)
    bf = lambda a: a.astype(BF16)

    bias_prompt, bias_last, bias_new = _bias_tables(rel_bias, tq, t_new)
    yp = x_prompt.reshape(nb * seq, d)
    ys = x_sample.reshape(ns * t_new, d)
    zero_state = jnp.zeros((nb, groups * state), F32)
    outs = {name: [] for name in ("kp", "vp", "ikp", "srp", "sip", "mkp", "mvp", "ks", "vs", "iks", "srs", "sis")}
    pool_kT = cache_k.transpose(0, 1, 3, 4, 2)
    pool_vT = cache_v.transpose(0, 1, 3, 4, 2)
    pool_ikT = cache_idx_k.transpose(0, 1, 3, 2)
    mem_k = cache_mem_k.reshape(depth, ns, n_mem, -1)
    mem_v = cache_mem_v.reshape(depth, ns, n_mem, -1)
    feature_major = lambda a, heads: a.reshape(nb, heads, -1, seq).transpose(0, 3, 1, 2)

    for l in range(depth):
        ws = _split_w_in(w_in[l])
        ssm_params = _ssm_params(ssm_a_re[l], ssm_a_im[l], ssm_b_re[l], ssm_b_im[l], ssm_c_re[l], ssm_c_im[l],
                                 ssm_log_dt[l])
        dense = (bf(w_up_attn[l]), bf(w_up_ssm[l]), bf(w_out[l]), norm_cross[l], bf(w_xq[l]))
        mlp_w = (bf(w_xo[l]), norm_mlp[l], bf(w_mlp_up[l]), bf(w_mlp_down[l]), norm_final)
        glu = bf(w_glu[l])

        qT, iqw, iwT, kT, vT, ikT, kb, vTb, ikb, u, gates = _proj_prompt(yp.reshape(nb, seq, d), norm_mix[l], ws, tm,
                                                                         tq)
        attn = _dsa_prompt(qT, iqw, iwT, ikb, kb, vTb, bias_prompt, tq)
        ssm, s_re, s_im = _ssm(u, zero_state, zero_state, ssm_params, ssm_d[l], glu, ts)
        yp, qc = _merge(yp, attn.reshape(nb * seq, -1), ssm.reshape(nb * seq, -1), gates.reshape(nb * seq, -1),
                        *dense, tm)
        mk, mv = _memkv(mem_prompt, norm_mem[l], bf(w_xk[l]), bf(w_xv[l]))
        oc = _cross(qc.reshape(nb, seq, -1), mk[None], mv[None], 0, tm)
        yp = _mlp(yp, oc.reshape(nb * seq, -1), *mlp_w, tm, l == depth - 1)
        outs["kp"].append(feature_major(kT, N_HEADS))
        outs["vp"].append(feature_major(vT, N_HEADS))
        outs["ikp"].append(ikT.transpose(0, 2, 1))
        outs["srp"].append(s_re.reshape(nb, groups, state))
        outs["sip"].append(s_im.reshape(nb, groups, state))
        outs["mkp"].append(mk.reshape(nb, n_mem, X_HEADS, X_HEAD_DIM))
        outs["mvp"].append(mv.reshape(nb, n_mem, X_HEADS, X_HEAD_DIM))

        q, iq, iw, k, v, ik, u, gates = _proj_rows(ys, norm_mix[l], ws)
        r3 = lambda a: a.reshape(ns, t_new, -1)
        attn = _dsa_sample(r3(q), r3(iq), r3(iw), r3(k), r3(v), r3(ik), pool_kT, pool_vT, pool_ikT, l,
                           page_table, bias_last, bias_new)
        ssm, s_re, s_im = _ssm(r3(u), state_ssm_re[l].reshape(ns, -1), state_ssm_im[l].reshape(ns, -1),
                               ssm_params, ssm_d[l], glu, t_new)
        ys, qc = _merge(ys, attn.reshape(ns * t_new, -1), ssm.reshape(ns * t_new, -1), gates, *dense, ns * t_new)
        oc = _cross(r3(qc), mem_k, mem_v, l, t_new)
        ys = _mlp(ys, oc.reshape(ns * t_new, -1), *mlp_w, ns * t_new, l == depth - 1)
        outs["ks"].append(k.reshape(ns, t_new, N_HEADS, HEAD_DIM))
        outs["vs"].append(v.reshape(ns, t_new, N_HEADS, HEAD_DIM))
        outs["iks"].append(ik.reshape(ns, t_new, IDX_DIM))
        outs["srs"].append(s_re.reshape(ns, groups, state))
        outs["sis"].append(s_im.reshape(ns, groups, state))

    st = lambda name: jnp.stack(outs[name])
    return (yp.reshape(nb, seq, d), ys.reshape(ns, t_new, d),
            st("kp"), st("vp"), st("ikp"), st("srp"), st("sip"), st("mkp"), st("mvp"),
            st("ks"), st("vs"), st("iks"), st("srs"), st("sis"))
```

```python
import functools
import math

import jax
import jax.numpy as jnp
import numpy as np
from jax import lax
from jax.experimental import pallas as pl
from jax.experimental.pallas import tpu as pltpu

F32 = jnp.float32
BF16 = jnp.bfloat16
I32 = jnp.int32

EPS = 1e-6
N_HEADS = 8
HEAD_DIM = 64
ATTN_WIDTH = N_HEADS * HEAD_DIM
IDX_HEADS = 8
IDX_DIM = 64
TOP_K_MAX = 256
SSM_GROUP = 16
SSM_STATE = 64
X_HEADS = 4
X_HEAD_DIM = 128
N_BUCKETS = 32
MAX_DISTANCE = 128
PAGE_SIZE = 128
V_ROWS = HEAD_DIM + 16

MXU_DEPTH = 256
LANES = 128
SUBLANES = 8
VMEM_LIMIT_BYTES = 56 * 1024 * 1024
INT_MIN = -(2 ** 31)
INT_MAX = 2 ** 31 - 1
EARLY_EXIT_BIT = 28
F32_MIN = float(np.finfo(np.float32).min)
F32_MAX = float(np.finfo(np.float32).max)
M_INIT = -1e30

NT_DIMS = (((1,), (1,)), ((), ()))


def _params(*semantics):
    return pltpu.CompilerParams(dimension_semantics=semantics, vmem_limit_bytes=VMEM_LIMIT_BYTES)


def _rmsnorm(x, g):
    ms = jnp.mean(x * x, axis=-1, keepdims=True)
    return x * lax.rsqrt(ms + EPS) * g


def _dot(a, b):
    return jnp.dot(a, b, preferred_element_type=F32)


def _dot_nt(a, b):
    return lax.dot_general(a, b, NT_DIMS, preferred_element_type=F32)


def _monotone_key(x):
    b = lax.bitcast_convert_type(x, I32)
    key = jnp.where(b < 0, b ^ INT_MAX, b)
    return jnp.where(x == 0.0, 0, key)


def _row_tile(rows, want):
    t = min(rows, want)
    assert rows % t == 0, (rows, t)
    return t


def _t5_bucket(dist):
    max_exact = N_BUCKETS // 2
    d = jnp.maximum(dist, 0)
    df = jnp.maximum(d, 1).astype(F32)
    large = max_exact + (jnp.log(df / max_exact) / math.log(MAX_DISTANCE / max_exact)
                         * (N_BUCKETS - max_exact)).astype(I32)
    large = jnp.minimum(large, N_BUCKETS - 1)
    return jnp.where(d < max_exact, d, large)


def _bias_lookup(rel_ref, bucket, h):
    out = jnp.zeros(bucket.shape, F32)
    for b in range(N_BUCKETS):
        out = jnp.where(bucket == b, rel_ref[b, h], out)
    return out - rel_ref[N_BUCKETS - 1, h]


def _bias_prompt_kernel(rel_ref, o_ref, *, tq):
    r = lax.broadcasted_iota(I32, (2 * tq, tq), 0)
    j = lax.broadcasted_iota(I32, (2 * tq, tq), 1)
    bucket = _t5_bucket(j - r + tq)
    for h in range(N_HEADS):
        o_ref[h] = _bias_lookup(rel_ref, bucket, h)


def _bias_sample_kernel(rel_ref, last_ref, new_ref, *, t_new):
    row = lax.broadcasted_iota(I32, (N_HEADS * t_new, PAGE_SIZE), 0)
    lane = lax.broadcasted_iota(I32, (N_HEADS * t_new, PAGE_SIZE), 1)
    last = jnp.zeros(row.shape, F32)
    new = jnp.zeros(row.shape, F32)
    for h in range(N_HEADS):
        t = row - h * t_new
        in_head = (t >= 0) & (t < t_new)
        last = jnp.where(in_head, _bias_lookup(rel_ref, _t5_bucket(PAGE_SIZE + t - lane), h), last)
        new = jnp.where(in_head, _bias_lookup(rel_ref, _t5_bucket(t - lane), h), new)
    last_ref[...] = last
    new_ref[...] = new


def _bias_tables(rel_bias, tq, t_new):
    smem = pl.BlockSpec(memory_space=pltpu.SMEM)
    prompt = pl.pallas_call(
        functools.partial(_bias_prompt_kernel, tq=tq),
        out_shape=jax.ShapeDtypeStruct((N_HEADS, 2 * tq, tq), F32),
        in_specs=[smem], name="bias_prompt")(rel_bias)
    last, new = pl.pallas_call(
        functools.partial(_bias_sample_kernel, t_new=t_new),
        out_shape=[jax.ShapeDtypeStruct((N_HEADS * t_new, PAGE_SIZE), F32)] * 2,
        in_specs=[smem], name="bias_sample")(rel_bias)
    return prompt, last, new


def _proj_prompt_kernel(x_ref, g_ref, wqT_ref, wiqT_ref, wiwT_ref, wk_ref, wkT_ref, wvT_ref, wik_ref, wikT_ref,
                        wu_ref, wg_ref,
                        qT_ref, iqw_ref, iwT_ref, kT_ref, vT_ref, ikT_ref, kb_ref, vTb_ref, ikb_ref, u_ref, gate_ref,
                        *, tq):
    h = _rmsnorm(x_ref[0], g_ref[...]).astype(BF16)
    qT_ref[0] = _dot_nt(wqT_ref[...], h).astype(BF16)
    iqT = _dot_nt(wiqT_ref[...], h).astype(BF16)
    for blk in range(iqw_ref.shape[1]):
        for hh in range(IDX_HEADS):
            iqw_ref[0, blk, :, hh * tq:(hh + 1) * tq] = iqT[hh * IDX_DIM:(hh + 1) * IDX_DIM, blk * tq:(blk + 1) * tq]
    iwT_ref[0] = _dot_nt(wiwT_ref[...], h) * (IDX_HEADS ** -0.5)
    kT_ref[0] = _dot_nt(wkT_ref[...], h)
    kb_ref[0] = _dot(h, wk_ref[...]).astype(BF16)
    vT = _dot_nt(wvT_ref[...], h)
    vT_ref[0] = vT
    ones = jnp.ones((V_ROWS - HEAD_DIM, vT.shape[1]), BF16)
    for hh in range(N_HEADS):
        vTb_ref[0, hh * V_ROWS:hh * V_ROWS + HEAD_DIM, :] = vT[hh * HEAD_DIM:(hh + 1) * HEAD_DIM, :].astype(BF16)
        vTb_ref[0, hh * V_ROWS + HEAD_DIM:(hh + 1) * V_ROWS, :] = ones
    ikT_ref[0] = _dot_nt(wikT_ref[...], h)
    ikb_ref[0] = _dot(h, wik_ref[...]).astype(BF16)
    u_ref[0] = _dot(h, wu_ref[...])
    gate_ref[0] = jax.nn.sigmoid(_dot(h, wg_ref[...]))


def _proj_rows_kernel(x_ref, g_ref, wq_ref, wiq_ref, wiw_ref, wk_ref, wv_ref, wik_ref, wu_ref, wg_ref,
                      q_ref, iq_ref, iw_ref, k_ref, v_ref, ik_ref, u_ref, gate_ref):
    h = _rmsnorm(x_ref[...], g_ref[...]).astype(BF16)
    q_ref[...] = _dot(h, wq_ref[...])
    iq_ref[...] = _dot(h, wiq_ref[...])
    iw_ref[...] = _dot(h, wiw_ref[...]) * (IDX_HEADS ** -0.5)
    k_ref[...] = _dot(h, wk_ref[...])
    v_ref[...] = _dot(h, wv_ref[...])
    ik_ref[...] = _dot(h, wik_ref[...])
    u_ref[...] = _dot(h, wu_ref[...])
    gate_ref[...] = jax.nn.sigmoid(_dot(h, wg_ref[...]))


def _split_w_in(w_in):
    sizes = (ATTN_WIDTH, ATTN_WIDTH, ATTN_WIDTH, IDX_HEADS * IDX_DIM, IDX_DIM, IDX_HEADS)
    offs = np.cumsum((0,) + sizes)
    d_model = w_in.shape[0]
    ssm_width = (w_in.shape[1] - offs[-1] - 2 * d_model)
    wq, wk, wv, wiq, wik, wiw = (w_in[:, offs[i]:offs[i + 1]] for i in range(6))
    wu = w_in[:, offs[-1]:offs[-1] + ssm_width]
    wg = w_in[:, offs[-1] + ssm_width:]
    wq = wq * (HEAD_DIM ** -0.5)
    wiq = wiq * (IDX_DIM ** -0.5)
    return tuple(w.astype(BF16) for w in (wq, wk, wv, wiq, wik, wiw, wu, wg))


def _full(shape):
    return pl.BlockSpec(shape, lambda *_: (0,) * len(shape))


def _proj_prompt(x, g, ws, tm, tq):
    n, t, d = x.shape
    wq, wk, wv, wiq, wik, wiw, wu, wg = ws
    sw, gw = wu.shape[1], wg.shape[1]
    assert tm % tq == 0
    weights = (wq.T, wiq.T, wiw.T, wk, wk.T, wv.T, wik, wik.T, wu, wg)
    rows = lambda w: pl.BlockSpec((1, tm, w), lambda b, i: (b, i, 0))
    cols = lambda w: pl.BlockSpec((1, w, tm), lambda b, i: (b, 0, i))
    iq_wide = IDX_HEADS * tq
    out_shape = [
        jax.ShapeDtypeStruct((n, ATTN_WIDTH, t), BF16),
        jax.ShapeDtypeStruct((n, t // tq, IDX_DIM, iq_wide), BF16),
        jax.ShapeDtypeStruct((n, IDX_HEADS, t), F32),
        jax.ShapeDtypeStruct((n, ATTN_WIDTH, t), F32),
        jax.ShapeDtypeStruct((n, ATTN_WIDTH, t), F32),
        jax.ShapeDtypeStruct((n, IDX_DIM, t), F32),
        jax.ShapeDtypeStruct((n, t, ATTN_WIDTH), BF16),
        jax.ShapeDtypeStruct((n, N_HEADS * V_ROWS, t), BF16),
        jax.ShapeDtypeStruct((n, t, IDX_DIM), BF16),
        jax.ShapeDtypeStruct((n, t, sw), F32),
        jax.ShapeDtypeStruct((n, t, gw), F32),
    ]
    out_specs = [cols(ATTN_WIDTH), pl.BlockSpec((1, tm // tq, IDX_DIM, iq_wide), lambda b, i: (b, i, 0, 0)),
                 cols(IDX_HEADS), cols(ATTN_WIDTH), cols(ATTN_WIDTH), cols(IDX_DIM), rows(ATTN_WIDTH),
                 cols(N_HEADS * V_ROWS), rows(IDX_DIM), rows(sw), rows(gw)]
    return pl.pallas_call(
        functools.partial(_proj_prompt_kernel, tq=tq), grid=(n, t // tm), out_shape=out_shape,
        in_specs=[rows(d), _full((1, d))] + [_full(w.shape) for w in weights],
        out_specs=out_specs, compiler_params=_params("parallel", "parallel"), name="proj_prompt",
    )(x, g.reshape(1, d), *weights)


def _proj_rows(x, g, ws):
    r, d = x.shape
    wq, wk, wv, wiq, wik, wiw, wu, wg = ws
    weights = (wq, wiq, wiw, wk, wv, wik, wu, wg)
    out_shape = [jax.ShapeDtypeStruct((r, w.shape[1]), F32) for w in weights]
    return pl.pallas_call(
        _proj_rows_kernel, out_shape=out_shape, compiler_params=_params(), name="proj_rows",
    )(x, g.reshape(1, d), *weights)


def _dsa_prompt_kernel(qT_ref, iqw_ref, iwT_ref, ikb_ref, kb_ref, vT_ref, bias_ref, o_ref,
                       s_ref, t_ref, pen_ref, pnear_ref, q2_ref, sbuf_ref, pbuf_ref, acc_ref, m_ref, l_ref,
                       *, tq, top_k):
    i = pl.program_id(1)
    gk = KEY_GROUP * tq
    n_grp = lax.div(i + KEY_GROUP, KEY_GROUP)
    n_far_grp = lax.div(jnp.maximum(i - 1, 0) + KEY_GROUP - 1, KEY_GROUP)
    iw = iwT_ref[0]
    row = lax.broadcasted_iota(I32, (tq, tq), 0)
    col = lax.broadcasted_iota(I32, (tq, tq), 1)
    grow = lax.broadcasted_iota(I32, (gk, tq), 0)

    ahead = row - col

    def score_group(g, has_future):
        for cc in range(KEY_GROUP):
            c = g * KEY_GROUP + cc
            r0 = pl.multiple_of(c * tq, tq)
            d = _dot(ikb_ref[0, pl.ds(r0, tq), :], iqw_ref[0, 0])
            sc = jnp.zeros((tq, tq), F32)
            for h in range(IDX_HEADS):
                sc = sc + iw[h:h + 1, :] * jnp.maximum(d[:, h * tq:(h + 1) * tq], 0.0)
            key = _monotone_key(sc)
            if has_future:
                key = jnp.where(ahead > (i - c) * tq, INT_MIN, key)
            s_ref[pl.ds(r0, tq), :] = key

    def past_group(g, carry):
        score_group(g, False)
        return carry

    lax.fori_loop(0, n_grp - 1, past_group, 0)
    score_group(n_grp - 1, True)

    def count(pred):
        def body(g, acc8):
            r0 = pl.multiple_of(g * gk, gk)
            ind = jnp.where(pred(s_ref[pl.ds(r0, gk), :], r0 + grow), 1, 0)
            return acc8 + ind.reshape(gk // SUBLANES, SUBLANES, tq).sum(axis=0)
        acc8 = lax.fori_loop(0, n_grp, body, jnp.zeros((SUBLANES, tq), I32))
        return acc8.sum(axis=0, keepdims=True)

    def bit_step(it, state):
        tu, cnt_ge = state
        cand_u = tu | lax.shift_left(jnp.int32(1), 31 - it)
        cand = cand_u ^ INT_MIN
        cnt = count(lambda blk, r0: blk >= cand)
        ok = cnt >= top_k
        return jnp.where(ok, cand_u, tu), jnp.where(ok, cnt, cnt_ge)

    state = lax.fori_loop(0, EARLY_EXIT_BIT, bit_step, (jnp.zeros((1, tq), I32), jnp.zeros((1, tq), I32)))
    t_ref[0:1, :], t_ref[1:2, :] = state
    unsettled = (state[1] != top_k) & (state[1] != 0)

    @pl.when(jnp.max(unsettled.astype(I32)) > 0)
    def _():
        t_ref[0:1, :], t_ref[1:2, :] = lax.fori_loop(EARLY_EXIT_BIT, 32, bit_step, state)

    tu, cnt_ge = t_ref[0:1, :], t_ref[1:2, :]
    thr = jnp.maximum(tu ^ INT_MIN, INT_MIN + 1)
    has_ties = jnp.max((cnt_ge > top_k).astype(I32)) > 0

    w_chunk = jnp.maximum(i - 1, 0)
    w0 = pl.multiple_of(w_chunk * tq, tq)

    @pl.when(jnp.logical_not(has_ties))
    def _():
        def pen_group(g, carry):
            r0 = pl.multiple_of(g * gk, gk)
            sel = (s_ref[pl.ds(r0, gk), :] >= thr) & (r0 + grow < (i - 1) * tq)
            pen_ref[pl.ds(r0, gk), :] = jnp.where(sel, 0.0, F32_MIN)
            return carry

        lax.fori_loop(0, jnp.maximum(n_far_grp, 1), pen_group, 0)
        pnear_ref[...] = jnp.where(s_ref[pl.ds(w0, 2 * tq), :] >= thr, 0.0, F32_MIN)

    @pl.when(has_ties)
    def _():
        need = (top_k - count(lambda blk, r0: blk > thr)).astype(F32)
        lower = jnp.where(row >= col, 1.0, 0.0).astype(BF16)
        def select_chunk(r0, seen):
            blk = s_ref[pl.ds(r0, tq), :]
            tie = blk == thr
            tie01 = jnp.where(tie, 1.0, 0.0)
            rank = _dot(lower, tie01.astype(BF16)) + seen
            seen = seen + tie01.reshape(tq // SUBLANES, SUBLANES, tq).sum(axis=0).sum(axis=0, keepdims=True)
            return (blk > thr) | (tie & (rank <= need)), seen

        def pen_group(g, carry):
            seen, seen_window = carry
            for cc in range(KEY_GROUP):
                c = g * KEY_GROUP + cc
                r0 = pl.multiple_of(c * tq, tq)
                seen_window = jnp.where(c == w_chunk, seen, seen_window)
                sel, seen = select_chunk(r0, seen)
                pen_ref[pl.ds(r0, tq), :] = jnp.where(sel & (c < i - 1), 0.0, F32_MIN)
            return seen, seen_window

        zeros = jnp.zeros((1, tq), F32)
        _, seen = lax.fori_loop(0, n_grp, pen_group, (zeros, zeros))
        for k in range(2):
            sel, seen = select_chunk(w0 + k * tq, seen)
            pnear_ref[k * tq:(k + 1) * tq, :] = jnp.where(sel, 0.0, F32_MIN)

    zero = jnp.zeros((HEAD_DIM, tq), BF16)
    for hp in range(N_HEADS // 2):
        lo = qT_ref[0, (2 * hp) * HEAD_DIM:(2 * hp + 1) * HEAD_DIM, :]
        hi = qT_ref[0, (2 * hp + 1) * HEAD_DIM:(2 * hp + 2) * HEAD_DIM, :]
        q2_ref[hp * 2 * HEAD_DIM:(hp + 1) * 2 * HEAD_DIM, :] = jnp.concatenate(
            [jnp.concatenate([lo, zero], axis=1), jnp.concatenate([zero, hi], axis=1)], axis=0)
    m_ref[...] = jnp.full(m_ref.shape, M_INIT, F32)
    l_ref[...] = jnp.zeros(l_ref.shape, F32)
    acc_ref[...] = jnp.zeros(acc_ref.shape, F32)

    def update_head(h, m8, r0, n_c):
        hs = slice(h * HEAD_DIM, (h + 1) * HEAD_DIM)
        ls = slice(h * SUBLANES, (h + 1) * SUBLANES)
        m_old = m_ref[h:h + 1, :]
        m_new = jnp.maximum(m_old, m8.max(axis=0, keepdims=True))
        alpha = jnp.exp(m_old - m_new)
        for cc in range(n_c):
            p = jnp.exp(sbuf_ref[h, cc * tq:(cc + 1) * tq, :] - m_new)
            pbuf_ref[h, cc * tq:(cc + 1) * tq, :] = p.astype(BF16)
        pv = _dot(vT_ref[0, h * V_ROWS:(h + 1) * V_ROWS, pl.ds(r0, n_c * tq)], pbuf_ref[h, 0:n_c * tq, :])
        acc_ref[hs, :] = alpha * acc_ref[hs, :] + pv[0:HEAD_DIM, :]
        l_ref[ls, :] = alpha * l_ref[ls, :] + pv[HEAD_DIM:HEAD_DIM + SUBLANES, :]
        m_ref[h:h + 1, :] = m_new

    def logits(hp, r0, n_c, pen_src, pen_r0, bias_rows):
        m8 = [None, None]
        for cc in range(n_c):
            pen = pen_src[pl.ds(pen_r0 + cc * tq, tq), :]
            kslab = kb_ref[0, pl.ds(r0 + cc * tq, tq), hp * 2 * HEAD_DIM:(hp + 1) * 2 * HEAD_DIM]
            s2 = _dot(kslab, q2_ref[hp * 2 * HEAD_DIM:(hp + 1) * 2 * HEAD_DIM, :])
            for e in range(2):
                h = 2 * hp + e
                s = s2[:, e * tq:(e + 1) * tq] + pen
                if bias_rows is not None:
                    s = s + bias_ref[h, pl.ds(bias_rows[cc], tq), :]
                sbuf_ref[h, cc * tq:(cc + 1) * tq, :] = s
                cm = s.reshape(tq // SUBLANES, SUBLANES, tq).max(axis=0)
                m8[e] = cm if m8[e] is None else jnp.maximum(m8[e], cm)
        return tuple(m8)

    def attend_group(m8, r0, n_c, pen_src, pen_r0, bias_rows, next_group_logits):
        pairs = N_HEADS // 2
        for hp in range(pairs):
            if hp + 1 < pairs:
                m8_next = logits(hp + 1, r0, n_c, pen_src, pen_r0, bias_rows)
            else:
                m8_next = next_group_logits() if next_group_logits is not None else None
            for e in range(2):
                update_head(2 * hp + e, m8[e], r0, n_c)
            m8 = m8_next
        return m8

    def far_logits0(g):
        r0 = pl.multiple_of(g * gk, gk)
        return logits(0, r0, KEY_GROUP, pen_ref, r0, None)

    def far_group(g, m8):
        r0 = pl.multiple_of(g * gk, gk)
        nxt = jnp.minimum(g + 1, n_far_grp - 1)
        return attend_group(m8, r0, KEY_GROUP, pen_ref, r0, None, lambda: far_logits0(nxt))

    lax.fori_loop(0, n_far_grp, far_group, far_logits0(0))
    first_bias = pl.multiple_of(jnp.where(i == 0, tq, 0), tq)
    near = (w0, 2, pnear_ref, 0, (first_bias, tq))
    attend_group(logits(0, *near), *near, None)

    for h in range(N_HEADS):
        hs = slice(h * HEAD_DIM, (h + 1) * HEAD_DIM)
        acc_ref[hs, :] = acc_ref[hs, :] / l_ref[h * SUBLANES:h * SUBLANES + 1, :]
    o_ref[0] = acc_ref[...].T.astype(BF16)


def _dsa_prompt(qT, iqw, iwT, ikb, kb, vT, bias, tq):
    n, _, t = qT.shape
    n_blk = t // tq
    assert t % tq == 0 and tq >= MAX_DISTANCE and n_blk % KEY_GROUP == 0 and n_blk >= 2, (t, tq)
    top_k = min(TOP_K_MAX, t // 4)
    gk = KEY_GROUP * tq
    colblk = lambda w: pl.BlockSpec((1, w, tq), lambda b, i: (b, 0, i))
    whole = lambda a: pl.BlockSpec((1,) + a.shape[1:], lambda b, i: (b, 0, 0), pipeline_mode=pl.Buffered(1))
    kern = functools.partial(_dsa_prompt_kernel, tq=tq, top_k=top_k)
    return pl.pallas_call(
        kern, grid=(n, n_blk),
        out_shape=jax.ShapeDtypeStruct((n, t, ATTN_WIDTH), BF16),
        in_specs=[colblk(ATTN_WIDTH), pl.BlockSpec((1, 1) + iqw.shape[2:], lambda b, i: (b, i, 0, 0)),
                  colblk(IDX_HEADS), whole(ikb), whole(kb), whole(vT),
                  pl.BlockSpec(bias.shape, lambda b, i: (0, 0, 0), pipeline_mode=pl.Buffered(1))],
        out_specs=pl.BlockSpec((1, tq, ATTN_WIDTH), lambda b, i: (b, i, 0)),
        scratch_shapes=[pltpu.VMEM((t, tq), I32),
                        pltpu.VMEM((SUBLANES, tq), I32),
                        pltpu.VMEM((t, tq), F32),
                        pltpu.VMEM((2 * tq, tq), F32),
                        pltpu.VMEM((N_HEADS * HEAD_DIM, 2 * tq), BF16),
                        pltpu.VMEM((N_HEADS, gk, tq), F32),
                        pltpu.VMEM((N_HEADS, gk, tq), BF16),
                        pltpu.VMEM((ATTN_WIDTH, tq), F32),
                        pltpu.VMEM((N_HEADS, tq), F32),
                        pltpu.VMEM((N_HEADS * SUBLANES, tq), F32)],
        compiler_params=_params("parallel", "arbitrary"), name="dsa_prompt",
    )(qT, iqw, iwT, ikb, kb, vT, bias)


def _page_specs(page_shape, layer, pages_per_step):
    def spec(r):
        return pl.BlockSpec((1, 1) + page_shape,
                            lambda b, g, pt: (layer, pt[b, g * pages_per_step + r]) + (0,) * len(page_shape))
    return [spec(r) for r in range(pages_per_step)]


def _head_sum(x, t_new):
    return x.reshape(IDX_HEADS, t_new, x.shape[-1]).sum(axis=0)


def _dsa_sample_score_kernel(pt_ref, *refs, pps, t_new):
    page_refs = refs[:pps]
    iq_ref, w_ref, iknew_ref, sp_ref, sn_ref = refs[pps:]
    g = pl.program_id(1)
    iq = iq_ref[0]
    w = w_ref[0]
    for r in range(pps):
        d = _dot(iq, page_refs[r][0, 0].astype(BF16))
        sc = _head_sum(w * jnp.maximum(d, 0.0), t_new)
        sp_ref[0, :, r * PAGE_SIZE:(r + 1) * PAGE_SIZE] = _monotone_key(sc)

    @pl.when(g == pl.num_programs(1) - 1)
    def _():
        d = _dot_nt(iq, iknew_ref[0].astype(BF16))
        sc = _head_sum(w * jnp.maximum(d, 0.0), t_new)
        t = lax.broadcasted_iota(I32, sc.shape, 0)
        j = lax.broadcasted_iota(I32, sc.shape, 1)
        sn_ref[0] = jnp.where(j > t, INT_MIN, _monotone_key(sc))


def _dsa_sample_thr_kernel(sp_ref, sn_ref, thr_ref, j_ref, *, top_k, idx_bits):
    rows, past = sp_ref.shape
    lane = lax.broadcasted_iota(I32, (rows, LANES), 1)

    def count(pred):
        tot = jnp.where(pred(sn_ref[...], past + lane), 1, 0)
        for c in range(past // LANES):
            tot = tot + jnp.where(pred(sp_ref[:, c * LANES:(c + 1) * LANES], c * LANES + lane), 1, 0)
        return jnp.broadcast_to(tot.sum(axis=-1, keepdims=True), (rows, LANES))

    def bit_step(it, tu):
        cand_u = tu | lax.shift_left(jnp.int32(1), 31 - it)
        cand = cand_u ^ INT_MIN
        cnt = count(lambda blk, idx: blk >= cand)
        return jnp.where(cnt >= top_k, cand_u, tu)

    tu = lax.fori_loop(0, 32, bit_step, jnp.zeros((rows, LANES), I32))
    thr = jnp.maximum(tu ^ INT_MIN, INT_MIN + 1)
    cnt_gt = count(lambda blk, idx: blk > thr)
    cnt_ge = count(lambda blk, idx: blk >= thr)
    need = top_k - cnt_gt
    multi = cnt_ge > top_k
    thr_ref[...] = thr
    j_ref[...] = jnp.full((rows, LANES), INT_MAX, I32)

    @pl.when(jnp.max(multi.astype(I32)) > 0)
    def _():
        def j_step(it, jv):
            cand = jv | lax.shift_left(jnp.int32(1), idx_bits - 1 - it)
            cnt = count(lambda blk, idx: (blk == thr) & (idx < cand))
            return jnp.where(cnt < need, cand, jv)
        jv = lax.fori_loop(0, idx_bits, j_step, jnp.zeros((rows, LANES), I32))
        j_ref[...] = jnp.where(multi, jv, INT_MAX)


def _dsa_sample_attn_kernel(pt_ref, *refs, pps, t_new):
    k_refs = refs[:pps]
    v_refs = refs[pps:2 * pps]
    (q_ref, sp_ref, sn_ref, thr_ref, j_ref, knew_ref, vnew_ref, blast_ref, bnew_ref,
     o_ref, acc_ref, m_ref, l_ref) = refs[2 * pps:]
    g = pl.program_id(1)
    last = g == pl.num_programs(1) - 1
    thr = thr_ref[0]
    jsel = j_ref[0]
    lane = lax.broadcasted_iota(I32, (t_new, PAGE_SIZE), 1)

    @pl.when(g == 0)
    def _():
        m_ref[...] = jnp.full(m_ref.shape, M_INIT, F32)
        l_ref[...] = jnp.zeros(l_ref.shape, F32)
        acc_ref[...] = jnp.zeros(acc_ref.shape, F32)

    def update(pages):
        logit = []
        for keys, idx, kT_ref, _, bias in pages:
            s = jnp.concatenate([_dot(q_ref[0, h], kT_ref[h].astype(BF16)) for h in range(N_HEADS)], axis=0)
            if bias is not None:
                s = s + bias
            sel = (keys > thr) | ((keys == thr) & (idx <= jsel))
            cap = jnp.where(sel, F32_MAX, F32_MIN)
            logit.append(jnp.minimum(s, jnp.concatenate([cap] * N_HEADS, axis=0)))
        m_old = m_ref[...]
        m_new = jnp.maximum(m_old, jnp.broadcast_to(functools.reduce(jnp.maximum, logit).max(axis=-1, keepdims=True),
                                                    m_old.shape))
        prob = [jnp.exp(s - m_new) for s in logit]
        alpha = jnp.exp(m_old - m_new)
        l_ref[...] = alpha * l_ref[...] + jnp.broadcast_to(sum(prob).sum(axis=-1, keepdims=True), m_old.shape)
        pv = sum(jnp.concatenate([_dot_nt(p[h * t_new:(h + 1) * t_new, :].astype(BF16), page[3][h].astype(BF16))
                                  for h in range(N_HEADS)], axis=0)
                 for p, page in zip(prob, pages))
        acc_ref[...] = alpha[:, 0:HEAD_DIM] * acc_ref[...] + pv
        m_ref[...] = m_new

    last_f = jnp.where(last, 1.0, 0.0)
    update([(sp_ref[0, :, r * PAGE_SIZE:(r + 1) * PAGE_SIZE], (g * pps + r) * PAGE_SIZE + lane,
             k_refs[r].at[0, 0], v_refs[r].at[0, 0],
             blast_ref[...] * last_f if r == pps - 1 else None)
            for r in range(pps)])

    @pl.when(last)
    def _():
        n_past = pl.num_programs(1) * pps * PAGE_SIZE
        update([(sn_ref[0], n_past + lane, knew_ref.at[0], vnew_ref.at[0], bnew_ref[...])])
        o_full = acc_ref[...] / l_ref[:, 0:HEAD_DIM]
        for h in range(N_HEADS):
            o_ref[0, h] = o_full[h * t_new:(h + 1) * t_new, :].astype(BF16)


def _dsa_sample(q, iq, iw, k_new, v_new, ik_new, pool_kT, pool_vT, pool_ikT, layer, page_table, bias_last,
                bias_new):
    n, t_new, _ = q.shape
    n_pages = page_table.shape[1]
    past = n_pages * PAGE_SIZE
    top_k = min(TOP_K_MAX, (past + t_new) // 4)
    ht = N_HEADS * t_new
    assert t_new <= PAGE_SIZE and IDX_HEADS == N_HEADS

    heads = lambda a, dim: a.reshape(n, t_new, N_HEADS, dim).transpose(0, 2, 1, 3)
    iq_hm = heads(iq, IDX_DIM).reshape(n, ht, IDX_DIM).astype(BF16)
    w_hm = jnp.broadcast_to(iw.transpose(0, 2, 1).reshape(n, ht, 1), (n, ht, LANES))
    q_hm = heads(q, HEAD_DIM).astype(BF16)
    page = lambda a: jnp.pad(heads(a, HEAD_DIM).transpose(0, 1, 3, 2),
                             ((0, 0), (0, 0), (0, 0), (0, PAGE_SIZE - t_new)))
    k_page, v_page = page(k_new), page(v_new)
    ik_pad = jnp.pad(ik_new, ((0, 0), (0, PAGE_SIZE - t_new), (0, 0)))

    per_b = lambda a: pl.BlockSpec((1,) + a.shape[1:], lambda b, g, pt: (b,) + (0,) * (a.ndim - 1))
    const = lambda a: pl.BlockSpec(a.shape, lambda b, g, pt: (0,) * a.ndim)
    sp_spec = lambda pages: pl.BlockSpec((1, t_new, pages * PAGE_SIZE), lambda b, g, pt: (b, 0, g))

    pps = math.gcd(n_pages, SCORE_PAGES_PER_STEP)
    sp, sn = pl.pallas_call(
        functools.partial(_dsa_sample_score_kernel, pps=pps, t_new=t_new),
        grid_spec=pltpu.PrefetchScalarGridSpec(
            num_scalar_prefetch=1, grid=(n, n_pages // pps),
            in_specs=_page_specs((IDX_DIM, PAGE_SIZE), layer, pps) + [per_b(iq_hm), per_b(w_hm), per_b(ik_pad)],
            out_specs=[sp_spec(pps), pl.BlockSpec((1, t_new, PAGE_SIZE), lambda b, g, pt: (b, 0, 0))]),
        out_shape=[jax.ShapeDtypeStruct((n, t_new, past), I32), jax.ShapeDtypeStruct((n, t_new, PAGE_SIZE), I32)],
        compiler_params=_params("parallel", "arbitrary"), name="dsa_sample_score",
    )(page_table, *([pool_ikT] * pps), iq_hm, w_hm, ik_pad)

    rows = n * t_new
    rt = _row_tile(rows, 32)
    idx_bits = int(math.ceil(math.log2(past + PAGE_SIZE)))
    thr, jsel = pl.pallas_call(
        functools.partial(_dsa_sample_thr_kernel, top_k=top_k, idx_bits=idx_bits),
        grid=(rows // rt,),
        in_specs=[pl.BlockSpec((rt, past), lambda i: (i, 0)), pl.BlockSpec((rt, PAGE_SIZE), lambda i: (i, 0))],
        out_specs=[pl.BlockSpec((rt, LANES), lambda i: (i, 0))] * 2,
        out_shape=[jax.ShapeDtypeStruct((rows, LANES), I32)] * 2,
        compiler_params=_params("parallel"), name="dsa_sample_thr",
    )(sp.reshape(rows, past), sn.reshape(rows, PAGE_SIZE))
    thr = thr.reshape(n, t_new, LANES)
    jsel = jsel.reshape(n, t_new, LANES)

    kv_page = (N_HEADS, HEAD_DIM, PAGE_SIZE)
    pps = math.gcd(n_pages, ATTN_PAGES_PER_STEP)
    o = pl.pallas_call(
        functools.partial(_dsa_sample_attn_kernel, pps=pps, t_new=t_new),
        grid_spec=pltpu.PrefetchScalarGridSpec(
            num_scalar_prefetch=1, grid=(n, n_pages // pps),
            in_specs=(_page_specs(kv_page, layer, pps) + _page_specs(kv_page, layer, pps)
                      + [per_b(q_hm), sp_spec(pps), per_b(sn), per_b(thr), per_b(jsel), per_b(k_page), per_b(v_page),
                         const(bias_last), const(bias_new)]),
            out_specs=pl.BlockSpec((1, N_HEADS, t_new, HEAD_DIM), lambda b, g, pt: (b, 0, 0, 0)),
            scratch_shapes=[pltpu.VMEM((ht, HEAD_DIM), F32), pltpu.VMEM((ht, LANES), F32),
                            pltpu.VMEM((ht, LANES), F32)]),
        out_shape=jax.ShapeDtypeStruct((n, N_HEADS, t_new, HEAD_DIM), BF16),
        compiler_params=_params("parallel", "arbitrary"), name="dsa_sample_attn",
    )(page_table, *([pool_kT] * pps), *([pool_vT] * pps), q_hm, sp, sn, thr, jsel, k_page, v_page, bias_last,
      bias_new)
    return o.transpose(0, 2, 1, 3).reshape(n, t_new, ATTN_WIDTH)


def _gelu_tanh(y):
    return 0.5 * y * (1.0 + jnp.tanh(math.sqrt(2.0 / math.pi) * (y + 0.044715 * (y * y * y))))


def _ssm_kernel(u_ref, x0r_ref, x0i_ref, lr_ref, li_ref, b_ref, c_ref, d_ref, wglu_ref,
                y_ref, sr_ref, si_ref, x_ref, st_ref, *, tile):
    j = pl.program_id(1)
    ns = lr_ref.shape[1]

    @pl.when(j == 0)
    def _():
        st_ref[0:1, :] = x0r_ref[0]
        st_ref[1:2, :] = x0i_ref[0]

    u = u_ref[0]
    ub = u.astype(BF16)
    clusters = max(1, u.shape[1] // MXU_DEPTH)
    cw = u.shape[1] // clusters
    sw = ns // clusters
    for k in range(clusters):
        for part in range(2):
            cols = slice(part * ns + k * sw, part * ns + (k + 1) * sw)
            x_ref[:, cols] = _dot(ub[:, k * cw:(k + 1) * cw], b_ref[k * cw:(k + 1) * cw, cols])
    lr = lr_ref[...]
    li = li_ref[...]

    def step(t, carry):
        sr, si = carry
        br = x_ref[pl.ds(t, 1), 0:ns]
        bi = x_ref[pl.ds(t, 1), ns:2 * ns]
        nr = lr * sr - li * si + br
        ni = lr * si + li * sr + bi
        x_ref[pl.ds(t, 1), 0:ns] = nr
        x_ref[pl.ds(t, 1), ns:2 * ns] = ni
        return nr, ni

    sr, si = lax.fori_loop(0, tile, step, (st_ref[0:1, :], st_ref[1:2, :]))
    st_ref[0:1, :] = sr
    st_ref[1:2, :] = si

    cx = []
    for k in range(clusters):
        ch = slice(k * cw, (k + 1) * cw)
        re = slice(k * sw, (k + 1) * sw)
        im = slice(ns + k * sw, ns + (k + 1) * sw)
        cx.append(_dot(x_ref[:, re].astype(BF16), c_ref[re, ch]) + _dot(x_ref[:, im].astype(BF16), c_ref[im, ch]))
    y = jnp.concatenate(cx, axis=1) + d_ref[...] * u
    y = _gelu_tanh(y)
    y_ref[0] = (y * jax.nn.sigmoid(_dot(y.astype(BF16), wglu_ref[...]))).astype(BF16)

    @pl.when(j == pl.num_programs(1) - 1)
    def _():
        sr_ref[0] = sr
        si_ref[0] = si


def _ssm_params(a_re, a_im, b_re, b_im, c_re, c_im, log_dt):
    g = a_re.shape[0]
    dt = jnp.exp(log_dt)[:, None]
    mag = jnp.exp(a_re * dt)
    lam_re, lam_im = mag * jnp.cos(a_im * dt), mag * jnp.sin(a_im * dt)
    den = a_re * a_re + a_im * a_im
    nr, ni = lam_re - 1.0, lam_im
    f_re = (nr * a_re + ni * a_im) / den
    f_im = (ni * a_re - nr * a_im) / den
    bb_re = f_re[..., None] * b_re - f_im[..., None] * b_im
    bb_im = f_re[..., None] * b_im + f_im[..., None] * b_re
    eye = jnp.eye(g, dtype=F32)
    p, c = bb_re.shape[1], bb_re.shape[2]
    blk_in = lambda m: jnp.einsum('gpc,gh->gchp', m, eye).reshape(g * c, g * p)
    blk_out = lambda m: jnp.einsum('gcp,gh->gphc', m, eye).reshape(g * p, g * c)
    b_blk = jnp.concatenate([blk_in(bb_re), blk_in(bb_im)], axis=1).astype(BF16)
    c_blk = jnp.concatenate([blk_out(c_re), -blk_out(c_im)], axis=0).astype(BF16)
    return lam_re.reshape(1, g * p), lam_im.reshape(1, g * p), b_blk, c_blk


def _ssm(u, x0_re, x0_im, params, d_skip, w_glu, tile):
    n, t, w = u.shape
    lam_re, lam_im, b_blk, c_blk = params
    ns = lam_re.shape[1]
    x0_re = x0_re.reshape(n, 1, ns)
    x0_im = x0_im.reshape(n, 1, ns)
    seq = lambda width: pl.BlockSpec((1, tile, width), lambda b, j: (b, j, 0))
    state = pl.BlockSpec((1, 1, ns), lambda b, j: (b, 0, 0))
    y, sr, si = pl.pallas_call(
        functools.partial(_ssm_kernel, tile=tile), grid=(n, t // tile),
        in_specs=[seq(w), state, state, _full((1, ns)), _full((1, ns)), _full(b_blk.shape), _full(c_blk.shape),
                  _full((1, w)), _full(w_glu.shape)],
        out_specs=[seq(w), state, state],
        out_shape=[jax.ShapeDtypeStruct((n, t, w), BF16), jax.ShapeDtypeStruct((n, 1, ns), F32),
                   jax.ShapeDtypeStruct((n, 1, ns), F32)],
        scratch_shapes=[pltpu.VMEM((tile, 2 * ns), F32), pltpu.VMEM((SUBLANES, ns), F32)],
        compiler_params=_params("parallel", "arbitrary"), name="ssm",
    )(u, x0_re, x0_im, lam_re, lam_im, b_blk, c_blk, d_skip.reshape(1, w), w_glu)
    return y, sr, si


def _merge_kernel(x_ref, attn_ref, ssm_ref, gate_ref, wua_ref, wus_ref, wo_ref, gx_ref, wxq_ref, xo_ref, qc_ref):
    d = x_ref.shape[-1]
    gate = gate_ref[...]
    mixed = gate[:, 0:d] * _dot(attn_ref[...], wua_ref[...]) + gate[:, d:2 * d] * _dot(ssm_ref[...], wus_ref[...])
    x = x_ref[...] + _dot(mixed.astype(BF16), wo_ref[...])
    xo_ref[...] = x
    qc_ref[...] = _dot(_rmsnorm(x, gx_ref[...]).astype(BF16), wxq_ref[...]).astype(BF16)


def _merge(x, attn, ssm, gates, w_up_attn, w_up_ssm, w_out, g_cross, w_xq, tm):
    r, d = x.shape
    rows = lambda w: pl.BlockSpec((tm, w), lambda i: (i, 0))
    weights = (w_up_attn, w_up_ssm, w_out, g_cross.reshape(1, d), w_xq)
    return pl.pallas_call(
        _merge_kernel, grid=(r // tm,),
        in_specs=[rows(d), rows(attn.shape[1]), rows(ssm.shape[1]), rows(gates.shape[1])]
        + [_full(w.shape) for w in weights],
        out_specs=[rows(d), rows(w_xq.shape[1])],
        out_shape=[jax.ShapeDtypeStruct((r, d), F32), jax.ShapeDtypeStruct((r, w_xq.shape[1]), BF16)],
        compiler_params=_params("parallel"), name="merge",
    )(x, attn, ssm, gates, *weights)


def _cross_kernel(q_ref, mk_ref, mv_ref, o_ref):
    q = q_ref[0]
    outs = []
    for h in range(X_HEADS):
        sl = slice(h * X_HEAD_DIM, (h + 1) * X_HEAD_DIM)
        s = _dot_nt(q[:, sl], mk_ref[0, 0, :, sl].astype(BF16)) * (X_HEAD_DIM ** -0.5)
        s = s - s.max(axis=-1, keepdims=True)
        p = jnp.exp(s)
        p = (p / p.sum(axis=-1, keepdims=True)).astype(BF16)
        outs.append(_dot(p, mv_ref[0, 0, :, sl].astype(BF16)))
    o_ref[0] = jnp.concatenate(outs, axis=-1).astype(BF16)


def _cross(q, mk, mv, layer, tq):
    n, t, w = q.shape
    mem = pl.BlockSpec((1, 1) + mk.shape[2:], lambda b, i: (layer, b, 0, 0))
    blk = pl.BlockSpec((1, tq, w), lambda b, i: (b, i, 0))
    return pl.pallas_call(
        _cross_kernel, grid=(n, t // tq), in_specs=[blk, mem, mem], out_specs=blk,
        out_shape=jax.ShapeDtypeStruct((n, t, w), BF16),
        compiler_params=_params("parallel", "parallel"), name="cross",
    )(q, mk, mv)


def _mlp_kernel(x_ref, oc_ref, wxo_ref, gm_ref, wup_ref, wdn_ref, gf_ref, o_ref, *, ff_chunk, final):
    x = x_ref[...] + _dot(oc_ref[...], wxo_ref[...])
    h = _rmsnorm(x, gm_ref[...]).astype(BF16)
    acc = jnp.zeros(x.shape, F32)
    for c in range(wup_ref.shape[1] // ff_chunk):
        sl = slice(c * ff_chunk, (c + 1) * ff_chunk)
        a = jnp.maximum(_dot(h, wup_ref[:, sl]), 0.0)
        acc = acc + _dot((a * a).astype(BF16), wdn_ref[sl, :])
    x = x + acc
    o_ref[...] = _rmsnorm(x, gf_ref[...]) if final else x


def _mlp(x, o_cross, w_xo, g_mlp, w_up, w_down, g_final, tm, final):
    r, d = x.shape
    rows = lambda w: pl.BlockSpec((tm, w), lambda i: (i, 0))
    weights = (w_xo, g_mlp.reshape(1, d), w_up, w_down, g_final.reshape(1, d))
    return pl.pallas_call(
        functools.partial(_mlp_kernel, ff_chunk=min(1024, w_up.shape[1]), final=final), grid=(r // tm,),
        in_specs=[rows(d), rows(o_cross.shape[1])] + [_full(w.shape) for w in weights],
        out_specs=rows(d), out_shape=jax.ShapeDtypeStruct((r, d), F32),
        compiler_params=_params("parallel"), name="mlp",
    )(x, o_cross, *weights)


def _memkv_kernel(mem_ref, g_ref, wk_ref, wv_ref, mk_ref, mv_ref):
    hm = _rmsnorm(mem_ref[0], g_ref[...]).astype(BF16)
    mk_ref[0] = _dot(hm, wk_ref[...])
    mv_ref[0] = _dot(hm, wv_ref[...])


def _memkv(mem, g, w_k, w_v):
    n, m, d = mem.shape
    w = w_k.shape[1]
    blk = lambda width: pl.BlockSpec((1, m, width), lambda b: (b, 0, 0))
    return pl.pallas_call(
        _memkv_kernel, grid=(n,),
        in_specs=[blk(d), _full((1, d)), _full(w_k.shape), _full(w_v.shape)],
        out_specs=[blk(w), blk(w)], out_shape=[jax.ShapeDtypeStruct((n, m, w), F32)] * 2,
        compiler_params=_params("parallel"), name="memkv",
    )(mem, g.reshape(1, d), w_k, w_v)


PROMPT_ROW_TILE = 256
PROMPT_Q_TILE = 128
KEY_GROUP = 4
SSM_TIME_TILE = 256
CROSS_Q_TILE = 512
SCORE_PAGES_PER_STEP = 32
ATTN_PAGES_PER_STEP = 16


def kernel(x_prompt, x_sample, mem_prompt, cache_k, cache_v, cache_idx_k, state_ssm_re, state_ssm_im,
           cache_mem_k, cache_mem_v, page_table, rel_bias, norm_mix, w_in, ssm_a_re, ssm_a_im,
           ssm_b_re, ssm_b_im, ssm_c_re, ssm_c_im, ssm_d, ssm_log_dt, w_glu, w_up_attn, w_up_ssm, w_out,
           norm_cross, norm_mem, w_xq, w_xk, w_xv, w_xo, norm_mlp, w_mlp_up, w_mlp_down, norm_final):
    depth = w_in.shape[0]
    nb, seq, d = x_prompt.shape
    ns, t_new, _ = x_sample.shape
    groups, state = ssm_a_re.shape[1], ssm_a_re.shape[2]
    n_mem = mem_prompt.shape[1]
    tm = _row_tile(seq, PROMPT_ROW_TILE)
    tq = _row_tile(seq, PROMPT_Q_TILE)
    ts = _row_tile(seq, SSM_TIME_TILE)
    bf = lambda a: a.astype(BF16)

    bias_prompt, bias_last, bias_new = _bias_tables(rel_bias, tq, t_new)
    yp = x_prompt.reshape(nb * seq, d)
    ys = x_sample.reshape(ns * t_new, d)
    zero_state = jnp.zeros((nb, groups * state), F32)
    outs = {name: [] for name in ("kp", "vp", "ikp", "srp", "sip", "mkp", "mvp", "ks", "vs", "iks", "srs", "sis")}
    pool_kT = cache_k.transpose(0, 1, 3, 4, 2)
    pool_vT = cache_v.transpose(0, 1, 3, 4, 2)
    pool_ikT = cache_idx_k.transpose(0, 1, 3, 2)
    mem_k = cache_mem_k.reshape(depth, ns, n_mem, -1)
    mem_v = cache_mem_v.reshape(depth, ns, n_mem, -1)
    feature_major = lambda a, heads: a.reshape(nb, heads, -1, seq).transpose(0, 3, 1, 2)

    for l in range(depth):
        ws = _split_w_in(w_in[l])
        ssm_params = _ssm_params(ssm_a_re[l], ssm_a_im[l], ssm_b_re[l], ssm_b_im[l], ssm_c_re[l], ssm_c_im[l],
                                 ssm_log_dt[l])
        dense = (bf(w_up_attn[l]), bf(w_up_ssm[l]), bf(w_out[l]), norm_cross[l], bf(w_xq[l]))
        mlp_w = (bf(w_xo[l]), norm_mlp[l], bf(w_mlp_up[l]), bf(w_mlp_down[l]), norm_final)
        glu = bf(w_glu[l])

        qT, iqw, iwT, kT, vT, ikT, kb, vTb, ikb, u, gates = _proj_prompt(yp.reshape(nb, seq, d), norm_mix[l], ws, tm,
                                                                         tq)
        attn = _dsa_prompt(qT, iqw, iwT, ikb, kb, vTb, bias_prompt, tq)
        ssm, s_re, s_im = _ssm(u, zero_state, zero_state, ssm_params, ssm_d[l], glu, ts)
        yp, qc = _merge(yp, attn.reshape(nb * seq, -1), ssm.reshape(nb * seq, -1), gates.reshape(nb * seq, -1),
                        *dense, tm)
        mk, mv = _memkv(mem_prompt, norm_mem[l], bf(w_xk[l]), bf(w_xv[l]))
        oc = _cross(qc.reshape(nb, seq, -1), mk[None], mv[None], 0, _row_tile(seq, CROSS_Q_TILE))
        yp = _mlp(yp, oc.reshape(nb * seq, -1), *mlp_w, tm, l == depth - 1)
        outs["kp"].append(feature_major(kT, N_HEADS))
        outs["vp"].append(feature_major(vT, N_HEADS))
        outs["ikp"].append(ikT.transpose(0, 2, 1))
        outs["srp"].append(s_re.reshape(nb, groups, state))
        outs["sip"].append(s_im.reshape(nb, groups, state))
        outs["mkp"].append(mk.reshape(nb, n_mem, X_HEADS, X_HEAD_DIM))
        outs["mvp"].append(mv.reshape(nb, n_mem, X_HEADS, X_HEAD_DIM))

        q, iq, iw, k, v, ik, u, gates = _proj_rows(ys, norm_mix[l], ws)
        r3 = lambda a: a.reshape(ns, t_new, -1)
        attn = _dsa_sample(r3(q), r3(iq), r3(iw), r3(k), r3(v), r3(ik), pool_kT, pool_vT, pool_ikT, l,
                           page_table, bias_last, bias_new)
        ssm, s_re, s_im = _ssm(r3(u), state_ssm_re[l].reshape(ns, -1), state_ssm_im[l].reshape(ns, -1),
                               ssm_params, ssm_d[l], glu, t_new)
        ys, qc = _merge(ys, attn.reshape(ns * t_new, -1), ssm.reshape(ns * t_new, -1), gates, *dense, ns * t_new)
        oc = _cross(r3(qc), mem_k, mem_v, l, t_new)
        ys = _mlp(ys, oc.reshape(ns * t_new, -1), *mlp_w, ns * t_new, l == depth - 1)
        outs["ks"].append(k.reshape(ns, t_new, N_HEADS, HEAD_DIM))
        outs["vs"].append(v.reshape(ns, t_new, N_HEADS, HEAD_DIM))
        outs["iks"].append(ik.reshape(ns, t_new, IDX_DIM))
        outs["srs"].append(s_re.reshape(ns, groups, state))
        outs["sis"].append(s_im.reshape(ns, groups, state))

    st = lambda name: jnp.stack(outs[name])
    return (yp.reshape(nb, seq, d), ys.reshape(ns, t_new, d),
            st("kp"), st("vp"), st("ikp"), st("srp"), st("sip"), st("mkp"), st("mvp"),
            st("ks"), st("vs"), st("iks"), st("srs"), st("sis"))
```

```python
import functools
import math

import jax
import jax.numpy as jnp
import numpy as np
from jax import lax
from jax.experimental import pallas as pl
from jax.experimental.pallas import tpu as pltpu

F32 = jnp.float32
BF16 = jnp.bfloat16
I32 = jnp.int32

EPS = 1e-6
N_HEADS = 8
HEAD_DIM = 64
ATTN_WIDTH = N_HEADS * HEAD_DIM
IDX_HEADS = 8
IDX_DIM = 64
TOP_K_MAX = 256
SSM_GROUP = 16
SSM_STATE = 64
X_HEADS = 4
X_HEAD_DIM = 128
N_BUCKETS = 32
MAX_DISTANCE = 128
PAGE_SIZE = 128
V_ROWS = HEAD_DIM + 16

MXU_DEPTH = 256
LANES = 128
SUBLANES = 8
VMEM_LIMIT_BYTES = 56 * 1024 * 1024
INT_MIN = -(2 ** 31)
INT_MAX = 2 ** 31 - 1
EARLY_EXIT_BIT = 28
F32_MIN = float(np.finfo(np.float32).min)
F32_MAX = float(np.finfo(np.float32).max)
M_INIT = -1e30

NT_DIMS = (((1,), (1,)), ((), ()))


def _params(*semantics):
    return pltpu.CompilerParams(dimension_semantics=semantics, vmem_limit_bytes=VMEM_LIMIT_BYTES)


def _rmsnorm(x, g):
    ms = jnp.mean(x * x, axis=-1, keepdims=True)
    return x * lax.rsqrt(ms + EPS) * g


def _dot(a, b):
    return jnp.dot(a, b, preferred_element_type=F32)


def _dot_nt(a, b):
    return lax.dot_general(a, b, NT_DIMS, preferred_element_type=F32)


def _monotone_key(x):
    b = lax.bitcast_convert_type(x, I32)
    key = jnp.where(b < 0, b ^ INT_MAX, b)
    return jnp.where(x == 0.0, 0, key)


def _row_tile(rows, want):
    t = min(rows, want)
    assert rows % t == 0, (rows, t)
    return t


def _t5_bucket(dist):
    max_exact = N_BUCKETS // 2
    d = jnp.maximum(dist, 0)
    df = jnp.maximum(d, 1).astype(F32)
    large = max_exact + (jnp.log(df / max_exact) / math.log(MAX_DISTANCE / max_exact)
                         * (N_BUCKETS - max_exact)).astype(I32)
    large = jnp.minimum(large, N_BUCKETS - 1)
    return jnp.where(d < max_exact, d, large)


def _bias_lookup(rel_ref, bucket, h):
    out = jnp.zeros(bucket.shape, F32)
    for b in range(N_BUCKETS):
        out = jnp.where(bucket == b, rel_ref[b, h], out)
    return out - rel_ref[N_BUCKETS - 1, h]


def _bias_prompt_kernel(rel_ref, o_ref, *, tq):
    r = lax.broadcasted_iota(I32, (2 * tq, tq), 0)
    j = lax.broadcasted_iota(I32, (2 * tq, tq), 1)
    bucket = _t5_bucket(j - r + tq)
    for h in range(N_HEADS):
        o_ref[h] = _bias_lookup(rel_ref, bucket, h)


def _bias_sample_kernel(rel_ref, last_ref, new_ref, *, t_new):
    row = lax.broadcasted_iota(I32, (N_HEADS * t_new, PAGE_SIZE), 0)
    lane = lax.broadcasted_iota(I32, (N_HEADS * t_new, PAGE_SIZE), 1)
    last = jnp.zeros(row.shape, F32)
    new = jnp.zeros(row.shape, F32)
    for h in range(N_HEADS):
        t = row - h * t_new
        in_head = (t >= 0) & (t < t_new)
        last = jnp.where(in_head, _bias_lookup(rel_ref, _t5_bucket(PAGE_SIZE + t - lane), h), last)
        new = jnp.where(in_head, _bias_lookup(rel_ref, _t5_bucket(t - lane), h), new)
    last_ref[...] = last
    new_ref[...] = new


def _bias_tables(rel_bias, tq, t_new):
    smem = pl.BlockSpec(memory_space=pltpu.SMEM)
    prompt = pl.pallas_call(
        functools.partial(_bias_prompt_kernel, tq=tq),
        out_shape=jax.ShapeDtypeStruct((N_HEADS, 2 * tq, tq), F32),
        in_specs=[smem], name="bias_prompt")(rel_bias)
    last, new = pl.pallas_call(
        functools.partial(_bias_sample_kernel, t_new=t_new),
        out_shape=[jax.ShapeDtypeStruct((N_HEADS * t_new, PAGE_SIZE), F32)] * 2,
        in_specs=[smem], name="bias_sample")(rel_bias)
    return prompt, last, new


def _proj_prompt_kernel(x_ref, g_ref, wqT_ref, wiqT_ref, wiwT_ref, wk_ref, wkT_ref, wvT_ref, wik_ref, wikT_ref,
                        wu_ref, wg_ref,
                        qT_ref, iqw_ref, iwT_ref, kT_ref, vT_ref, ikT_ref, kb_ref, vTb_ref, ikb_ref, u_ref, gate_ref,
                        *, tq):
    h = _rmsnorm(x_ref[0], g_ref[...]).astype(BF16)
    qT_ref[0] = _dot_nt(wqT_ref[...], h).astype(BF16)
    iqT = _dot_nt(wiqT_ref[...], h).astype(BF16)
    for blk in range(iqw_ref.shape[1]):
        for hh in range(IDX_HEADS):
            iqw_ref[0, blk, :, hh * tq:(hh + 1) * tq] = iqT[hh * IDX_DIM:(hh + 1) * IDX_DIM, blk * tq:(blk + 1) * tq]
    iwT_ref[0] = _dot_nt(wiwT_ref[...], h) * (IDX_HEADS ** -0.5)
    kT_ref[0] = _dot_nt(wkT_ref[...], h)
    kb_ref[0] = _dot(h, wk_ref[...]).astype(BF16)
    vT = _dot_nt(wvT_ref[...], h)
    vT_ref[0] = vT
    ones = jnp.ones((V_ROWS - HEAD_DIM, vT.shape[1]), BF16)
    for hh in range(N_HEADS):
        vTb_ref[0, hh * V_ROWS:hh * V_ROWS + HEAD_DIM, :] = vT[hh * HEAD_DIM:(hh + 1) * HEAD_DIM, :].astype(BF16)
        vTb_ref[0, hh * V_ROWS + HEAD_DIM:(hh + 1) * V_ROWS, :] = ones
    ikT_ref[0] = _dot_nt(wikT_ref[...], h)
    ikb_ref[0] = _dot(h, wik_ref[...]).astype(BF16)
    u_ref[0] = _dot(h, wu_ref[...])
    gate_ref[0] = jax.nn.sigmoid(_dot(h, wg_ref[...]))


def _proj_rows_kernel(x_ref, g_ref, wq_ref, wiq_ref, wiw_ref, wk_ref, wv_ref, wik_ref, wu_ref, wg_ref,
                      q_ref, iq_ref, iw_ref, k_ref, v_ref, ik_ref, u_ref, gate_ref):
    h = _rmsnorm(x_ref[...], g_ref[...]).astype(BF16)
    q_ref[...] = _dot(h, wq_ref[...])
    iq_ref[...] = _dot(h, wiq_ref[...])
    iw_ref[...] = _dot(h, wiw_ref[...]) * (IDX_HEADS ** -0.5)
    k_ref[...] = _dot(h, wk_ref[...])
    v_ref[...] = _dot(h, wv_ref[...])
    ik_ref[...] = _dot(h, wik_ref[...])
    u_ref[...] = _dot(h, wu_ref[...])
    gate_ref[...] = jax.nn.sigmoid(_dot(h, wg_ref[...]))


def _split_w_in(w_in):
    sizes = (ATTN_WIDTH, ATTN_WIDTH, ATTN_WIDTH, IDX_HEADS * IDX_DIM, IDX_DIM, IDX_HEADS)
    offs = np.cumsum((0,) + sizes)
    d_model = w_in.shape[0]
    ssm_width = (w_in.shape[1] - offs[-1] - 2 * d_model)
    wq, wk, wv, wiq, wik, wiw = (w_in[:, offs[i]:offs[i + 1]] for i in range(6))
    wu = w_in[:, offs[-1]:offs[-1] + ssm_width]
    wg = w_in[:, offs[-1] + ssm_width:]
    wq = wq * (HEAD_DIM ** -0.5)
    wiq = wiq * (IDX_DIM ** -0.5)
    return tuple(w.astype(BF16) for w in (wq, wk, wv, wiq, wik, wiw, wu, wg))


def _full(shape):
    return pl.BlockSpec(shape, lambda *_: (0,) * len(shape))


def _proj_prompt(x, g, ws, tm, tq):
    n, t, d = x.shape
    wq, wk, wv, wiq, wik, wiw, wu, wg = ws
    sw, gw = wu.shape[1], wg.shape[1]
    assert tm % tq == 0
    weights = (wq.T, wiq.T, wiw.T, wk, wk.T, wv.T, wik, wik.T, wu, wg)
    rows = lambda w: pl.BlockSpec((1, tm, w), lambda b, i: (b, i, 0))
    cols = lambda w: pl.BlockSpec((1, w, tm), lambda b, i: (b, 0, i))
    iq_wide = IDX_HEADS * tq
    out_shape = [
        jax.ShapeDtypeStruct((n, ATTN_WIDTH, t), BF16),
        jax.ShapeDtypeStruct((n, t // tq, IDX_DIM, iq_wide), BF16),
        jax.ShapeDtypeStruct((n, IDX_HEADS, t), F32),
        jax.ShapeDtypeStruct((n, ATTN_WIDTH, t), F32),
        jax.ShapeDtypeStruct((n, ATTN_WIDTH, t), F32),
        jax.ShapeDtypeStruct((n, IDX_DIM, t), F32),
        jax.ShapeDtypeStruct((n, t, ATTN_WIDTH), BF16),
        jax.ShapeDtypeStruct((n, N_HEADS * V_ROWS, t), BF16),
        jax.ShapeDtypeStruct((n, t, IDX_DIM), BF16),
        jax.ShapeDtypeStruct((n, t, sw), F32),
        jax.ShapeDtypeStruct((n, t, gw), F32),
    ]
    out_specs = [cols(ATTN_WIDTH), pl.BlockSpec((1, tm // tq, IDX_DIM, iq_wide), lambda b, i: (b, i, 0, 0)),
                 cols(IDX_HEADS), cols(ATTN_WIDTH), cols(ATTN_WIDTH), cols(IDX_DIM), rows(ATTN_WIDTH),
                 cols(N_HEADS * V_ROWS), rows(IDX_DIM), rows(sw), rows(gw)]
    return pl.pallas_call(
        functools.partial(_proj_prompt_kernel, tq=tq), grid=(n, t // tm), out_shape=out_shape,
        in_specs=[rows(d), _full((1, d))] + [_full(w.shape) for w in weights],
        out_specs=out_specs, compiler_params=_params("parallel", "parallel"), name="proj_prompt",
    )(x, g.reshape(1, d), *weights)


def _proj_rows(x, g, ws):
    r, d = x.shape
    wq, wk, wv, wiq, wik, wiw, wu, wg = ws
    weights = (wq, wiq, wiw, wk, wv, wik, wu, wg)
    out_shape = [jax.ShapeDtypeStruct((r, w.shape[1]), F32) for w in weights]
    return pl.pallas_call(
        _proj_rows_kernel, out_shape=out_shape, compiler_params=_params(), name="proj_rows",
    )(x, g.reshape(1, d), *weights)


def _dsa_prompt_kernel(qT_ref, iqw_ref, iwT_ref, ikb_ref, kb_ref, vT_ref, bias_ref, o_ref,
                       s_ref, t_ref, pen_ref, pnear_ref, q2_ref, sbuf_ref, pbuf_ref, acc_ref, m_ref, l_ref,
                       *, tq, top_k):
    i = pl.program_id(1)
    gk = KEY_GROUP * tq
    n_grp = lax.div(i + KEY_GROUP, KEY_GROUP)
    n_far_grp = lax.div(jnp.maximum(i - 1, 0) + KEY_GROUP - 1, KEY_GROUP)
    iw = iwT_ref[0]
    row = lax.broadcasted_iota(I32, (tq, tq), 0)
    col = lax.broadcasted_iota(I32, (tq, tq), 1)
    grow = lax.broadcasted_iota(I32, (gk, tq), 0)

    ahead = row - col

    def score_group(g, has_future):
        for cc in range(KEY_GROUP):
            c = g * KEY_GROUP + cc
            r0 = pl.multiple_of(c * tq, tq)
            d = _dot(ikb_ref[0, pl.ds(r0, tq), :], iqw_ref[0, 0])
            sc = jnp.zeros((tq, tq), F32)
            for h in range(IDX_HEADS):
                sc = sc + iw[h:h + 1, :] * jnp.maximum(d[:, h * tq:(h + 1) * tq], 0.0)
            key = _monotone_key(sc)
            if has_future:
                key = jnp.where(ahead > (i - c) * tq, INT_MIN, key)
            s_ref[pl.ds(r0, tq), :] = key

    def past_group(g, carry):
        score_group(g, False)
        return carry

    lax.fori_loop(0, n_grp - 1, past_group, 0)
    score_group(n_grp - 1, True)

    def count(pred):
        def body(g, acc8):
            r0 = pl.multiple_of(g * gk, gk)
            ind = jnp.where(pred(s_ref[pl.ds(r0, gk), :], r0 + grow), 1, 0)
            return acc8 + ind.reshape(gk // SUBLANES, SUBLANES, tq).sum(axis=0)
        acc8 = lax.fori_loop(0, n_grp, body, jnp.zeros((SUBLANES, tq), I32))
        return acc8.sum(axis=0, keepdims=True)

    def bit_step(it, state):
        tu, cnt_ge = state
        cand_u = tu | lax.shift_left(jnp.int32(1), 31 - it)
        cand = cand_u ^ INT_MIN
        cnt = count(lambda blk, r0: blk >= cand)
        ok = cnt >= top_k
        return jnp.where(ok, cand_u, tu), jnp.where(ok, cnt, cnt_ge)

    state = lax.fori_loop(0, EARLY_EXIT_BIT, bit_step, (jnp.zeros((1, tq), I32), jnp.zeros((1, tq), I32)))
    t_ref[0:1, :], t_ref[1:2, :] = state
    unsettled = (state[1] != top_k) & (state[1] != 0)

    @pl.when(jnp.max(unsettled.astype(I32)) > 0)
    def _():
        t_ref[0:1, :], t_ref[1:2, :] = lax.fori_loop(EARLY_EXIT_BIT, 32, bit_step, state)

    tu, cnt_ge = t_ref[0:1, :], t_ref[1:2, :]
    thr = jnp.maximum(tu ^ INT_MIN, INT_MIN + 1)
    has_ties = jnp.max((cnt_ge > top_k).astype(I32)) > 0

    w_chunk = jnp.maximum(i - 1, 0)
    w0 = pl.multiple_of(w_chunk * tq, tq)

    @pl.when(jnp.logical_not(has_ties))
    def _():
        def pen_group(g, carry):
            r0 = pl.multiple_of(g * gk, gk)
            sel = (s_ref[pl.ds(r0, gk), :] >= thr) & (r0 + grow < (i - 1) * tq)
            pen_ref[pl.ds(r0, gk), :] = jnp.where(sel, 0.0, F32_MIN)
            return carry

        lax.fori_loop(0, jnp.maximum(n_far_grp, 1), pen_group, 0)
        pnear_ref[...] = jnp.where(s_ref[pl.ds(w0, 2 * tq), :] >= thr, 0.0, F32_MIN)

    @pl.when(has_ties)
    def _():
        need = (top_k - count(lambda blk, r0: blk > thr)).astype(F32)
        lower = jnp.where(row >= col, 1.0, 0.0).astype(BF16)
        def select_chunk(r0, seen):
            blk = s_ref[pl.ds(r0, tq), :]
            tie = blk == thr
            tie01 = jnp.where(tie, 1.0, 0.0)
            rank = _dot(lower, tie01.astype(BF16)) + seen
            seen = seen + tie01.reshape(tq // SUBLANES, SUBLANES, tq).sum(axis=0).sum(axis=0, keepdims=True)
            return (blk > thr) | (tie & (rank <= need)), seen

        def pen_group(g, carry):
            seen, seen_window = carry
            for cc in range(KEY_GROUP):
                c = g * KEY_GROUP + cc
                r0 = pl.multiple_of(c * tq, tq)
                seen_window = jnp.where(c == w_chunk, seen, seen_window)
                sel, seen = select_chunk(r0, seen)
                pen_ref[pl.ds(r0, tq), :] = jnp.where(sel & (c < i - 1), 0.0, F32_MIN)
            return seen, seen_window

        zeros = jnp.zeros((1, tq), F32)
        _, seen = lax.fori_loop(0, n_grp, pen_group, (zeros, zeros))
        for k in range(2):
            sel, seen = select_chunk(w0 + k * tq, seen)
            pnear_ref[k * tq:(k + 1) * tq, :] = jnp.where(sel, 0.0, F32_MIN)

    zero = jnp.zeros((HEAD_DIM, tq), BF16)
    for hp in range(N_HEADS // 2):
        lo = qT_ref[0, (2 * hp) * HEAD_DIM:(2 * hp + 1) * HEAD_DIM, :]
        hi = qT_ref[0, (2 * hp + 1) * HEAD_DIM:(2 * hp + 2) * HEAD_DIM, :]
        q2_ref[hp * 2 * HEAD_DIM:(hp + 1) * 2 * HEAD_DIM, :] = jnp.concatenate(
            [jnp.concatenate([lo, zero], axis=1), jnp.concatenate([zero, hi], axis=1)], axis=0)
    m_ref[...] = jnp.full(m_ref.shape, M_INIT, F32)
    l_ref[...] = jnp.zeros(l_ref.shape, F32)
    acc_ref[...] = jnp.zeros(acc_ref.shape, F32)

    def update_head(h, m8, r0, n_c):
        hs = slice(h * HEAD_DIM, (h + 1) * HEAD_DIM)
        ls = slice(h * SUBLANES, (h + 1) * SUBLANES)
        m_old = m_ref[h:h + 1, :]
        m_new = jnp.maximum(m_old, m8.max(axis=0, keepdims=True))
        alpha = jnp.exp(m_old - m_new)
        for cc in range(n_c):
            p = jnp.exp(sbuf_ref[h, cc * tq:(cc + 1) * tq, :] - m_new)
            pbuf_ref[h, cc * tq:(cc + 1) * tq, :] = p.astype(BF16)
        pv = _dot(vT_ref[0, h * V_ROWS:(h + 1) * V_ROWS, pl.ds(r0, n_c * tq)], pbuf_ref[h, 0:n_c * tq, :])
        acc_ref[hs, :] = alpha * acc_ref[hs, :] + pv[0:HEAD_DIM, :]
        l_ref[ls, :] = alpha * l_ref[ls, :] + pv[HEAD_DIM:HEAD_DIM + SUBLANES, :]
        m_ref[h:h + 1, :] = m_new

    def logits(hp, r0, n_c, pen_src, pen_r0, bias_rows):
        m8 = [None, None]
        for cc in range(n_c):
            pen = pen_src[pl.ds(pen_r0 + cc * tq, tq), :]
            kslab = kb_ref[0, pl.ds(r0 + cc * tq, tq), hp * 2 * HEAD_DIM:(hp + 1) * 2 * HEAD_DIM]
            s2 = _dot(kslab, q2_ref[hp * 2 * HEAD_DIM:(hp + 1) * 2 * HEAD_DIM, :])
            for e in range(2):
                h = 2 * hp + e
                s = s2[:, e * tq:(e + 1) * tq] + pen
                if bias_rows is not None:
                    s = s + bias_ref[h, pl.ds(bias_rows[cc], tq), :]
                sbuf_ref[h, cc * tq:(cc + 1) * tq, :] = s
                cm = s.reshape(tq // SUBLANES, SUBLANES, tq).max(axis=0)
                m8[e] = cm if m8[e] is None else jnp.maximum(m8[e], cm)
        return tuple(m8)

    def attend_group(m8, r0, n_c, pen_src, pen_r0, bias_rows, next_group_logits):
        pairs = N_HEADS // 2
        for hp in range(pairs):
            if hp + 1 < pairs:
                m8_next = logits(hp + 1, r0, n_c, pen_src, pen_r0, bias_rows)
            else:
                m8_next = next_group_logits() if next_group_logits is not None else None
            for e in range(2):
                update_head(2 * hp + e, m8[e], r0, n_c)
            m8 = m8_next
        return m8

    def far_logits0(g):
        r0 = pl.multiple_of(g * gk, gk)
        return logits(0, r0, KEY_GROUP, pen_ref, r0, None)

    def far_group(g, m8):
        r0 = pl.multiple_of(g * gk, gk)
        nxt = jnp.minimum(g + 1, n_far_grp - 1)
        return attend_group(m8, r0, KEY_GROUP, pen_ref, r0, None, lambda: far_logits0(nxt))

    lax.fori_loop(0, n_far_grp, far_group, far_logits0(0))
    first_bias = pl.multiple_of(jnp.where(i == 0, tq, 0), tq)
    near = (w0, 2, pnear_ref, 0, (first_bias, tq))
    attend_group(logits(0, *near), *near, None)

    for h in range(N_HEADS):
        hs = slice(h * HEAD_DIM, (h + 1) * HEAD_DIM)
        acc_ref[hs, :] = acc_ref[hs, :] / l_ref[h * SUBLANES:h * SUBLANES + 1, :]
    o_ref[0] = acc_ref[...].T.astype(BF16)


def _dsa_prompt(qT, iqw, iwT, ikb, kb, vT, bias, tq):
    n, _, t = qT.shape
    n_blk = t // tq
    assert t % tq == 0 and tq >= MAX_DISTANCE and n_blk % KEY_GROUP == 0 and n_blk >= 2, (t, tq)
    top_k = min(TOP_K_MAX, t // 4)
    gk = KEY_GROUP * tq
    colblk = lambda w: pl.BlockSpec((1, w, tq), lambda b, i: (b, 0, i))
    whole = lambda a: pl.BlockSpec((1,) + a.shape[1:], lambda b, i: (b, 0, 0), pipeline_mode=pl.Buffered(1))
    kern = functools.partial(_dsa_prompt_kernel, tq=tq, top_k=top_k)
    return pl.pallas_call(
        kern, grid=(n, n_blk),
        out_shape=jax.ShapeDtypeStruct((n, t, ATTN_WIDTH), BF16),
        in_specs=[colblk(ATTN_WIDTH), pl.BlockSpec((1, 1) + iqw.shape[2:], lambda b, i: (b, i, 0, 0)),
                  colblk(IDX_HEADS), whole(ikb), whole(kb), whole(vT),
                  pl.BlockSpec(bias.shape, lambda b, i: (0, 0, 0), pipeline_mode=pl.Buffered(1))],
        out_specs=pl.BlockSpec((1, tq, ATTN_WIDTH), lambda b, i: (b, i, 0)),
        scratch_shapes=[pltpu.VMEM((t, tq), I32),
                        pltpu.VMEM((SUBLANES, tq), I32),
                        pltpu.VMEM((t, tq), F32),
                        pltpu.VMEM((2 * tq, tq), F32),
                        pltpu.VMEM((N_HEADS * HEAD_DIM, 2 * tq), BF16),
                        pltpu.VMEM((N_HEADS, gk, tq), F32),
                        pltpu.VMEM((N_HEADS, gk, tq), BF16),
                        pltpu.VMEM((ATTN_WIDTH, tq), F32),
                        pltpu.VMEM((N_HEADS, tq), F32),
                        pltpu.VMEM((N_HEADS * SUBLANES, tq), F32)],
        compiler_params=_params("parallel", "arbitrary"), name="dsa_prompt",
    )(qT, iqw, iwT, ikb, kb, vT, bias)


def _page_specs(page_shape, layer, pages_per_step):
    def spec(r):
        return pl.BlockSpec((1, 1) + page_shape,
                            lambda b, g, pt: (layer, pt[b, g * pages_per_step + r]) + (0,) * len(page_shape))
    return [spec(r) for r in range(pages_per_step)]


def _head_sum(x, t_new):
    return x.reshape(IDX_HEADS, t_new, x.shape[-1]).sum(axis=0)


def _dsa_sample_score_kernel(pt_ref, *refs, pps, t_new):
    page_refs = refs[:pps]
    iq_ref, w_ref, iknew_ref, sp_ref, sn_ref = refs[pps:]
    g = pl.program_id(1)
    iq = iq_ref[0]
    w = w_ref[0]
    for r in range(pps):
        d = _dot(iq, page_refs[r][0, 0].astype(BF16))
        sc = _head_sum(w * jnp.maximum(d, 0.0), t_new)
        sp_ref[0, :, r * PAGE_SIZE:(r + 1) * PAGE_SIZE] = _monotone_key(sc)

    @pl.when(g == pl.num_programs(1) - 1)
    def _():
        d = _dot_nt(iq, iknew_ref[0].astype(BF16))
        sc = _head_sum(w * jnp.maximum(d, 0.0), t_new)
        t = lax.broadcasted_iota(I32, sc.shape, 0)
        j = lax.broadcasted_iota(I32, sc.shape, 1)
        sn_ref[0] = jnp.where(j > t, INT_MIN, _monotone_key(sc))


def _dsa_sample_thr_kernel(sp_ref, sn_ref, thr_ref, j_ref, *, top_k, idx_bits):
    rows, past = sp_ref.shape
    lane = lax.broadcasted_iota(I32, (rows, LANES), 1)

    def count(pred):
        tot = jnp.where(pred(sn_ref[...], past + lane), 1, 0)
        for c in range(past // LANES):
            tot = tot + jnp.where(pred(sp_ref[:, c * LANES:(c + 1) * LANES], c * LANES + lane), 1, 0)
        return jnp.broadcast_to(tot.sum(axis=-1, keepdims=True), (rows, LANES))

    def bit_step(it, tu):
        cand_u = tu | lax.shift_left(jnp.int32(1), 31 - it)
        cand = cand_u ^ INT_MIN
        cnt = count(lambda blk, idx: blk >= cand)
        return jnp.where(cnt >= top_k, cand_u, tu)

    tu = lax.fori_loop(0, 32, bit_step, jnp.zeros((rows, LANES), I32))
    thr = jnp.maximum(tu ^ INT_MIN, INT_MIN + 1)
    cnt_gt = count(lambda blk, idx: blk > thr)
    cnt_ge = count(lambda blk, idx: blk >= thr)
    need = top_k - cnt_gt
    multi = cnt_ge > top_k
    thr_ref[...] = thr
    j_ref[...] = jnp.full((rows, LANES), INT_MAX, I32)

    @pl.when(jnp.max(multi.astype(I32)) > 0)
    def _():
        def j_step(it, jv):
            cand = jv | lax.shift_left(jnp.int32(1), idx_bits - 1 - it)
            cnt = count(lambda blk, idx: (blk == thr) & (idx < cand))
            return jnp.where(cnt < need, cand, jv)
        jv = lax.fori_loop(0, idx_bits, j_step, jnp.zeros((rows, LANES), I32))
        j_ref[...] = jnp.where(multi, jv, INT_MAX)


def _dsa_sample_attn_kernel(pt_ref, *refs, pps, t_new):
    k_refs = refs[:pps]
    v_refs = refs[pps:2 * pps]
    (q_ref, sp_ref, sn_ref, thr_ref, j_ref, knew_ref, vnew_ref, blast_ref, bnew_ref,
     o_ref, acc_ref, m_ref, l_ref) = refs[2 * pps:]
    g = pl.program_id(1)
    last = g == pl.num_programs(1) - 1
    thr = thr_ref[0]
    jsel = j_ref[0]
    lane = lax.broadcasted_iota(I32, (t_new, PAGE_SIZE), 1)

    @pl.when(g == 0)
    def _():
        m_ref[...] = jnp.full(m_ref.shape, M_INIT, F32)
        l_ref[...] = jnp.zeros(l_ref.shape, F32)
        acc_ref[...] = jnp.zeros(acc_ref.shape, F32)

    lane_tiles = ATTN_WIDTH // LANES
    q = q_ref[0]

    def update(pages):
        logit = []
        for keys, idx, kT, _, bias in pages:
            s = _dot(q, kT.astype(BF16))
            if bias is not None:
                s = s + bias
            sel = (keys > thr) | ((keys == thr) & (idx <= jsel))
            cap = jnp.where(sel, F32_MAX, F32_MIN)
            logit.append(jnp.minimum(s, jnp.concatenate([cap] * N_HEADS, axis=0)))
        m_old = m_ref[...]
        m_new = jnp.maximum(m_old, jnp.broadcast_to(functools.reduce(jnp.maximum, logit).max(axis=-1, keepdims=True),
                                                    m_old.shape))
        prob = [jnp.exp(s - m_new) for s in logit]
        alpha = jnp.exp(m_old - m_new)
        l_ref[...] = alpha * l_ref[...] + jnp.broadcast_to(sum(prob).sum(axis=-1, keepdims=True), m_old.shape)
        pv = sum(_dot_nt(p.astype(BF16), page[3].astype(BF16)) for p, page in zip(prob, pages))
        acc_ref[...] = jnp.concatenate([alpha] * lane_tiles, axis=1) * acc_ref[...] + pv
        m_ref[...] = m_new

    last_f = jnp.where(last, 1.0, 0.0)
    update([(sp_ref[0, :, r * PAGE_SIZE:(r + 1) * PAGE_SIZE], (g * pps + r) * PAGE_SIZE + lane,
             k_refs[r][0, 0], v_refs[r][0, 0],
             blast_ref[...] * last_f if r == pps - 1 else None)
            for r in range(pps)])

    @pl.when(last)
    def _():
        n_past = pl.num_programs(1) * pps * PAGE_SIZE
        update([(sn_ref[0], n_past + lane, knew_ref[0], vnew_ref[0], bnew_ref[...])])
        o_full = acc_ref[...] / jnp.concatenate([l_ref[...]] * lane_tiles, axis=1)
        out_lane = lax.broadcasted_iota(I32, (t_new, ATTN_WIDTH), 1)
        out = jnp.zeros((t_new, ATTN_WIDTH), F32)
        for h in range(N_HEADS):
            own = (out_lane >= h * HEAD_DIM) & (out_lane < (h + 1) * HEAD_DIM)
            out = jnp.where(own, o_full[h * t_new:(h + 1) * t_new, :], out)
        o_ref[0] = out.astype(BF16)


def _dsa_sample(q, iq, iw, k_new, v_new, ik_new, pool_kT, pool_vT, pool_ikT, layer, page_table, bias_last,
                bias_new):
    n, t_new, _ = q.shape
    n_pages = page_table.shape[1]
    past = n_pages * PAGE_SIZE
    top_k = min(TOP_K_MAX, (past + t_new) // 4)
    ht = N_HEADS * t_new
    assert t_new <= PAGE_SIZE and IDX_HEADS == N_HEADS

    heads = lambda a, dim: a.reshape(n, t_new, N_HEADS, dim).transpose(0, 2, 1, 3)
    iq_hm = heads(iq, IDX_DIM).reshape(n, ht, IDX_DIM).astype(BF16)
    w_hm = jnp.broadcast_to(iw.transpose(0, 2, 1).reshape(n, ht, 1), (n, ht, LANES))
    eye = jnp.eye(N_HEADS, dtype=F32)
    q_bd = (heads(q, HEAD_DIM)[:, :, :, None, :] * eye[None, :, None, :, None]
            ).reshape(n, ht, ATTN_WIDTH).astype(BF16)
    page = lambda a: jnp.pad(heads(a, HEAD_DIM).transpose(0, 1, 3, 2).reshape(n, ATTN_WIDTH, t_new),
                             ((0, 0), (0, 0), (0, PAGE_SIZE - t_new)))
    k_page, v_page = page(k_new), page(v_new)
    ik_pad = jnp.pad(ik_new, ((0, 0), (0, PAGE_SIZE - t_new), (0, 0)))

    per_b = lambda a: pl.BlockSpec((1,) + a.shape[1:], lambda b, g, pt: (b,) + (0,) * (a.ndim - 1))
    const = lambda a: pl.BlockSpec(a.shape, lambda b, g, pt: (0,) * a.ndim)
    sp_spec = lambda pages: pl.BlockSpec((1, t_new, pages * PAGE_SIZE), lambda b, g, pt: (b, 0, g))

    pps = math.gcd(n_pages, SCORE_PAGES_PER_STEP)
    sp, sn = pl.pallas_call(
        functools.partial(_dsa_sample_score_kernel, pps=pps, t_new=t_new),
        grid_spec=pltpu.PrefetchScalarGridSpec(
            num_scalar_prefetch=1, grid=(n, n_pages // pps),
            in_specs=_page_specs((IDX_DIM, PAGE_SIZE), layer, pps) + [per_b(iq_hm), per_b(w_hm), per_b(ik_pad)],
            out_specs=[sp_spec(pps), pl.BlockSpec((1, t_new, PAGE_SIZE), lambda b, g, pt: (b, 0, 0))]),
        out_shape=[jax.ShapeDtypeStruct((n, t_new, past), I32), jax.ShapeDtypeStruct((n, t_new, PAGE_SIZE), I32)],
        compiler_params=_params("parallel", "arbitrary"), name="dsa_sample_score",
    )(page_table, *([pool_ikT] * pps), iq_hm, w_hm, ik_pad)

    rows = n * t_new
    rt = _row_tile(rows, THR_ROW_TILE)
    idx_bits = int(math.ceil(math.log2(past + PAGE_SIZE)))
    thr, jsel = pl.pallas_call(
        functools.partial(_dsa_sample_thr_kernel, top_k=top_k, idx_bits=idx_bits),
        grid=(rows // rt,),
        in_specs=[pl.BlockSpec((rt, past), lambda i: (i, 0)), pl.BlockSpec((rt, PAGE_SIZE), lambda i: (i, 0))],
        out_specs=[pl.BlockSpec((rt, LANES), lambda i: (i, 0))] * 2,
        out_shape=[jax.ShapeDtypeStruct((rows, LANES), I32)] * 2,
        compiler_params=_params("parallel"), name="dsa_sample_thr",
    )(sp.reshape(rows, past), sn.reshape(rows, PAGE_SIZE))
    thr = thr.reshape(n, t_new, LANES)
    jsel = jsel.reshape(n, t_new, LANES)

    kv_page = (ATTN_WIDTH, PAGE_SIZE)
    pps = math.gcd(n_pages, ATTN_PAGES_PER_STEP)
    return pl.pallas_call(
        functools.partial(_dsa_sample_attn_kernel, pps=pps, t_new=t_new),
        grid_spec=pltpu.PrefetchScalarGridSpec(
            num_scalar_prefetch=1, grid=(n, n_pages // pps),
            in_specs=(_page_specs(kv_page, layer, pps) + _page_specs(kv_page, layer, pps)
                      + [per_b(q_bd), sp_spec(pps), per_b(sn), per_b(thr), per_b(jsel), per_b(k_page), per_b(v_page),
                         const(bias_last), const(bias_new)]),
            out_specs=pl.BlockSpec((1, t_new, ATTN_WIDTH), lambda b, g, pt: (b, 0, 0)),
            scratch_shapes=[pltpu.VMEM((ht, ATTN_WIDTH), F32), pltpu.VMEM((ht, LANES), F32),
                            pltpu.VMEM((ht, LANES), F32)]),
        out_shape=jax.ShapeDtypeStruct((n, t_new, ATTN_WIDTH), BF16),
        compiler_params=_params("parallel", "arbitrary"), name="dsa_sample_attn",
    )(page_table, *([pool_kT] * pps), *([pool_vT] * pps), q_bd, sp, sn, thr, jsel, k_page, v_page, bias_last,
      bias_new)


def _gelu_tanh(y):
    return 0.5 * y * (1.0 + jnp.tanh(math.sqrt(2.0 / math.pi) * (y + 0.044715 * (y * y * y))))


def _ssm_kernel(u_ref, x0r_ref, x0i_ref, lr_ref, li_ref, b_ref, c_ref, d_ref, wglu_ref,
                y_ref, sr_ref, si_ref, x_ref, st_ref, *, tile):
    j = pl.program_id(1)
    ns = lr_ref.shape[1]
    seqs = range(u_ref.shape[0])

    @pl.when(j == 0)
    def _():
        for b in seqs:
            st_ref[b, 0:1, :] = x0r_ref[b]
            st_ref[b, 1:2, :] = x0i_ref[b]

    width = u_ref.shape[2]
    clusters = max(1, width // MXU_DEPTH)
    cw = width // clusters
    sw = ns // clusters
    for b in seqs:
        ub = u_ref[b].astype(BF16)
        for k in range(clusters):
            for part in range(2):
                cols = slice(part * ns + k * sw, part * ns + (k + 1) * sw)
                x_ref[b, :, cols] = _dot(ub[:, k * cw:(k + 1) * cw], b_ref[k * cw:(k + 1) * cw, cols])
    lr = lr_ref[...]
    li = li_ref[...]

    def step(t, carry):
        out = []
        for b in seqs:
            sr, si = carry[b]
            br = x_ref[b, pl.ds(t, 1), 0:ns]
            bi = x_ref[b, pl.ds(t, 1), ns:2 * ns]
            nr = lr * sr - li * si + br
            ni = lr * si + li * sr + bi
            x_ref[b, pl.ds(t, 1), 0:ns] = nr
            x_ref[b, pl.ds(t, 1), ns:2 * ns] = ni
            out.append((nr, ni))
        return tuple(out)

    final = lax.fori_loop(0, tile, step, tuple((st_ref[b, 0:1, :], st_ref[b, 1:2, :]) for b in seqs))
    for b in seqs:
        st_ref[b, 0:1, :] = final[b][0]
        st_ref[b, 1:2, :] = final[b][1]

    for b in seqs:
        cx = []
        for k in range(clusters):
            ch = slice(k * cw, (k + 1) * cw)
            re = slice(k * sw, (k + 1) * sw)
            im = slice(ns + k * sw, ns + (k + 1) * sw)
            cx.append(_dot(x_ref[b, :, re].astype(BF16), c_ref[re, ch])
                      + _dot(x_ref[b, :, im].astype(BF16), c_ref[im, ch]))
        y = jnp.concatenate(cx, axis=1) + d_ref[...] * u_ref[b]
        y = _gelu_tanh(y)
        y_ref[b] = (y * jax.nn.sigmoid(_dot(y.astype(BF16), wglu_ref[...]))).astype(BF16)

    @pl.when(j == pl.num_programs(1) - 1)
    def _():
        for b in seqs:
            sr_ref[b] = final[b][0]
            si_ref[b] = final[b][1]


def _ssm_params(a_re, a_im, b_re, b_im, c_re, c_im, log_dt):
    g = a_re.shape[0]
    dt = jnp.exp(log_dt)[:, None]
    mag = jnp.exp(a_re * dt)
    lam_re, lam_im = mag * jnp.cos(a_im * dt), mag * jnp.sin(a_im * dt)
    den = a_re * a_re + a_im * a_im
    nr, ni = lam_re - 1.0, lam_im
    f_re = (nr * a_re + ni * a_im) / den
    f_im = (ni * a_re - nr * a_im) / den
    bb_re = f_re[..., None] * b_re - f_im[..., None] * b_im
    bb_im = f_re[..., None] * b_im + f_im[..., None] * b_re
    eye = jnp.eye(g, dtype=F32)
    p, c = bb_re.shape[1], bb_re.shape[2]
    blk_in = lambda m: jnp.einsum('gpc,gh->gchp', m, eye).reshape(g * c, g * p)
    blk_out = lambda m: jnp.einsum('gcp,gh->gphc', m, eye).reshape(g * p, g * c)
    b_blk = jnp.concatenate([blk_in(bb_re), blk_in(bb_im)], axis=1).astype(BF16)
    c_blk = jnp.concatenate([blk_out(c_re), -blk_out(c_im)], axis=0).astype(BF16)
    return lam_re.reshape(1, g * p), lam_im.reshape(1, g * p), b_blk, c_blk


def _ssm(u, x0_re, x0_im, params, d_skip, w_glu, tile):
    n, t, w = u.shape
    lam_re, lam_im, b_blk, c_blk = params
    ns = lam_re.shape[1]
    x0_re = x0_re.reshape(n, 1, ns)
    x0_im = x0_im.reshape(n, 1, ns)
    nseq = math.gcd(n, SSM_SEQS_PER_STEP)
    seq = lambda width: pl.BlockSpec((nseq, tile, width), lambda b, j: (b, j, 0))
    state = pl.BlockSpec((nseq, 1, ns), lambda b, j: (b, 0, 0))
    y, sr, si = pl.pallas_call(
        functools.partial(_ssm_kernel, tile=tile), grid=(n // nseq, t // tile),
        in_specs=[seq(w), state, state, _full((1, ns)), _full((1, ns)), _full(b_blk.shape), _full(c_blk.shape),
                  _full((1, w)), _full(w_glu.shape)],
        out_specs=[seq(w), state, state],
        out_shape=[jax.ShapeDtypeStruct((n, t, w), BF16), jax.ShapeDtypeStruct((n, 1, ns), F32),
                   jax.ShapeDtypeStruct((n, 1, ns), F32)],
        scratch_shapes=[pltpu.VMEM((nseq, tile, 2 * ns), F32), pltpu.VMEM((nseq, SUBLANES, ns), F32)],
        compiler_params=_params("parallel", "arbitrary"), name="ssm",
    )(u, x0_re, x0_im, lam_re, lam_im, b_blk, c_blk, d_skip.reshape(1, w), w_glu)
    return y, sr, si


def _merge_kernel(x_ref, attn_ref, ssm_ref, gate_ref, wua_ref, wus_ref, wo_ref, gx_ref, wxq_ref, xo_ref, qc_ref):
    d = x_ref.shape[-1]
    gate = gate_ref[...]
    mixed = gate[:, 0:d] * _dot(attn_ref[...], wua_ref[...]) + gate[:, d:2 * d] * _dot(ssm_ref[...], wus_ref[...])
    x = x_ref[...] + _dot(mixed.astype(BF16), wo_ref[...])
    xo_ref[...] = x
    qc_ref[...] = _dot(_rmsnorm(x, gx_ref[...]).astype(BF16), wxq_ref[...]).astype(BF16)


def _merge(x, attn, ssm, gates, w_up_attn, w_up_ssm, w_out, g_cross, w_xq, tm):
    r, d = x.shape
    rows = lambda w: pl.BlockSpec((tm, w), lambda i: (i, 0))
    weights = (w_up_attn, w_up_ssm, w_out, g_cross.reshape(1, d), w_xq)
    return pl.pallas_call(
        _merge_kernel, grid=(r // tm,),
        in_specs=[rows(d), rows(attn.shape[1]), rows(ssm.shape[1]), rows(gates.shape[1])]
        + [_full(w.shape) for w in weights],
        out_specs=[rows(d), rows(w_xq.shape[1])],
        out_shape=[jax.ShapeDtypeStruct((r, d), F32), jax.ShapeDtypeStruct((r, w_xq.shape[1]), BF16)],
        compiler_params=_params("parallel"), name="merge",
    )(x, attn, ssm, gates, *weights)


def _cross_kernel(q_ref, mk_ref, mv_ref, o_ref):
    q = q_ref[0]
    outs = []
    for h in range(X_HEADS):
        sl = slice(h * X_HEAD_DIM, (h + 1) * X_HEAD_DIM)
        head_rows = pl.ds(h, mk_ref.shape[2] // X_HEADS, stride=X_HEADS)
        s = _dot_nt(q[:, sl], mk_ref[0, 0, head_rows, :].astype(BF16)) * (X_HEAD_DIM ** -0.5)
        s = s - s.max(axis=-1, keepdims=True)
        p = jnp.exp(s)
        p = (p / p.sum(axis=-1, keepdims=True)).astype(BF16)
        outs.append(_dot(p, mv_ref[0, 0, head_rows, :].astype(BF16)))
    o_ref[0] = jnp.concatenate(outs, axis=-1).astype(BF16)


def _cross(q, mk, mv, layer, tq):
    n, t, w = q.shape
    mem = pl.BlockSpec((1, 1) + mk.shape[2:], lambda b, i: (layer, b, 0, 0))
    blk = pl.BlockSpec((1, tq, w), lambda b, i: (b, i, 0))
    return pl.pallas_call(
        _cross_kernel, grid=(n, t // tq), in_specs=[blk, mem, mem], out_specs=blk,
        out_shape=jax.ShapeDtypeStruct((n, t, w), BF16),
        compiler_params=_params("parallel", "parallel"), name="cross",
    )(q, mk, mv)


def _mlp_kernel(x_ref, oc_ref, wxo_ref, gm_ref, wup_ref, wdn_ref, gf_ref, o_ref, *, ff_chunk, final):
    x = x_ref[...] + _dot(oc_ref[...], wxo_ref[...])
    h = _rmsnorm(x, gm_ref[...]).astype(BF16)
    acc = jnp.zeros(x.shape, F32)
    for c in range(wup_ref.shape[1] // ff_chunk):
        sl = slice(c * ff_chunk, (c + 1) * ff_chunk)
        a = jnp.maximum(_dot(h, wup_ref[:, sl]), 0.0)
        acc = acc + _dot((a * a).astype(BF16), wdn_ref[sl, :])
    x = x + acc
    o_ref[...] = _rmsnorm(x, gf_ref[...]) if final else x


def _mlp(x, o_cross, w_xo, g_mlp, w_up, w_down, g_final, tm, final):
    r, d = x.shape
    rows = lambda w: pl.BlockSpec((tm, w), lambda i: (i, 0))
    weights = (w_xo, g_mlp.reshape(1, d), w_up, w_down, g_final.reshape(1, d))
    return pl.pallas_call(
        functools.partial(_mlp_kernel, ff_chunk=min(1024, w_up.shape[1]), final=final), grid=(r // tm,),
        in_specs=[rows(d), rows(o_cross.shape[1])] + [_full(w.shape) for w in weights],
        out_specs=rows(d), out_shape=jax.ShapeDtypeStruct((r, d), F32),
        compiler_params=_params("parallel"), name="mlp",
    )(x, o_cross, *weights)


def _memkv_kernel(mem_ref, g_ref, wk_ref, wv_ref, mk_ref, mv_ref):
    hm = _rmsnorm(mem_ref[0], g_ref[...]).astype(BF16)
    mk = _dot(hm, wk_ref[...])
    mv = _dot(hm, wv_ref[...])
    for h in range(X_HEADS):
        head_rows = pl.ds(h, hm.shape[0], stride=X_HEADS)
        mk_ref[0, head_rows, :] = mk[:, h * X_HEAD_DIM:(h + 1) * X_HEAD_DIM]
        mv_ref[0, head_rows, :] = mv[:, h * X_HEAD_DIM:(h + 1) * X_HEAD_DIM]


def _memkv(mem, g, w_k, w_v):
    n, m, d = mem.shape
    w = w_k.shape[1]
    assert w == X_HEADS * X_HEAD_DIM
    out_blk = pl.BlockSpec((1, m * X_HEADS, X_HEAD_DIM), lambda b: (b, 0, 0))
    return pl.pallas_call(
        _memkv_kernel, grid=(n,),
        in_specs=[pl.BlockSpec((1, m, d), lambda b: (b, 0, 0)), _full((1, d)), _full(w_k.shape), _full(w_v.shape)],
        out_specs=[out_blk, out_blk], out_shape=[jax.ShapeDtypeStruct((n, m * X_HEADS, X_HEAD_DIM), F32)] * 2,
        compiler_params=_params("parallel"), name="memkv",
    )(mem, g.reshape(1, d), w_k, w_v)


PROMPT_ROW_TILE = 256
PROMPT_Q_TILE = 128
KEY_GROUP = 4
SSM_TIME_TILE = 256
SSM_SEQS_PER_STEP = 8
CROSS_Q_TILE = 512
THR_ROW_TILE = 128
SCORE_PAGES_PER_STEP = 32
ATTN_PAGES_PER_STEP = 16


def kernel(x_prompt, x_sample, mem_prompt, cache_k, cache_v, cache_idx_k, state_ssm_re, state_ssm_im,
           cache_mem_k, cache_mem_v, page_table, rel_bias, norm_mix, w_in, ssm_a_re, ssm_a_im,
           ssm_b_re, ssm_b_im, ssm_c_re, ssm_c_im, ssm_d, ssm_log_dt, w_glu, w_up_attn, w_up_ssm, w_out,
           norm_cross, norm_mem, w_xq, w_xk, w_xv, w_xo, norm_mlp, w_mlp_up, w_mlp_down, norm_final):
    depth = w_in.shape[0]
    nb, seq, d = x_prompt.shape
    ns, t_new, _ = x_sample.shape
    groups, state = ssm_a_re.shape[1], ssm_a_re.shape[2]
    n_mem = mem_prompt.shape[1]
    tm = _row_tile(seq, PROMPT_ROW_TILE)
    tq = _row_tile(seq, PROMPT_Q_TILE)
    ts = _row_tile(seq, SSM_TIME_TILE)
    bf = lambda a: a.astype(BF16)

    bias_prompt, bias_last, bias_new = _bias_tables(rel_bias, tq, t_new)
    yp = x_prompt.reshape(nb * seq, d)
    ys = x_sample.reshape(ns * t_new, d)
    zero_state = jnp.zeros((nb, groups * state), F32)
    outs = {name: [] for name in ("kp", "vp", "ikp", "srp", "sip", "mkp", "mvp", "ks", "vs", "iks", "srs", "sis")}
    kv_pool = lambda c: c.transpose(0, 1, 3, 4, 2).reshape(c.shape[:2] + (ATTN_WIDTH, PAGE_SIZE))
    pool_kT = kv_pool(cache_k)
    pool_vT = kv_pool(cache_v)
    pool_ikT = cache_idx_k.transpose(0, 1, 3, 2)
    mem_k = cache_mem_k.reshape(depth, ns, n_mem * X_HEADS, X_HEAD_DIM)
    mem_v = cache_mem_v.reshape(depth, ns, n_mem * X_HEADS, X_HEAD_DIM)
    feature_major = lambda a, heads: a.reshape(nb, heads, -1, seq).transpose(0, 3, 1, 2)

    for l in range(depth):
        ws = _split_w_in(w_in[l])
        ssm_params = _ssm_params(ssm_a_re[l], ssm_a_im[l], ssm_b_re[l], ssm_b_im[l], ssm_c_re[l], ssm_c_im[l],
                                 ssm_log_dt[l])
        dense = (bf(w_up_attn[l]), bf(w_up_ssm[l]), bf(w_out[l]), norm_cross[l], bf(w_xq[l]))
        mlp_w = (bf(w_xo[l]), norm_mlp[l], bf(w_mlp_up[l]), bf(w_mlp_down[l]), norm_final)
        glu = bf(w_glu[l])

        qT, iqw, iwT, kT, vT, ikT, kb, vTb, ikb, u, gates = _proj_prompt(yp.reshape(nb, seq, d), norm_mix[l], ws, tm,
                                                                         tq)
        attn = _dsa_prompt(qT, iqw, iwT, ikb, kb, vTb, bias_prompt, tq)
        ssm, s_re, s_im = _ssm(u, zero_state, zero_state, ssm_params, ssm_d[l], glu, ts)
        yp, qc = _merge(yp, attn.reshape(nb * seq, -1), ssm.reshape(nb * seq, -1), gates.reshape(nb * seq, -1),
                        *dense, tm)
        mk, mv = _memkv(mem_prompt, norm_mem[l], bf(w_xk[l]), bf(w_xv[l]))
        oc = _cross(qc.reshape(nb, seq, -1), mk[None], mv[None], 0, _row_tile(seq, CROSS_Q_TILE))
        yp = _mlp(yp, oc.reshape(nb * seq, -1), *mlp_w, tm, l == depth - 1)
        outs["kp"].append(feature_major(kT, N_HEADS))
        outs["vp"].append(feature_major(vT, N_HEADS))
        outs["ikp"].append(ikT.transpose(0, 2, 1))
        outs["srp"].append(s_re.reshape(nb, groups, state))
        outs["sip"].append(s_im.reshape(nb, groups, state))
        outs["mkp"].append(mk.reshape(nb, n_mem, X_HEADS, X_HEAD_DIM))
        outs["mvp"].append(mv.reshape(nb, n_mem, X_HEADS, X_HEAD_DIM))

        q, iq, iw, k, v, ik, u, gates = _proj_rows(ys, norm_mix[l], ws)
        r3 = lambda a: a.reshape(ns, t_new, -1)
        attn = _dsa_sample(r3(q), r3(iq), r3(iw), r3(k), r3(v), r3(ik), pool_kT, pool_vT, pool_ikT, l,
                           page_table, bias_last, bias_new)
        ssm, s_re, s_im = _ssm(r3(u), state_ssm_re[l].reshape(ns, -1), state_ssm_im[l].reshape(ns, -1),
                               ssm_params, ssm_d[l], glu, t_new)
        ys, qc = _merge(ys, attn.reshape(ns * t_new, -1), ssm.reshape(ns * t_new, -1), gates, *dense, ns * t_new)
        oc = _cross(r3(qc), mem_k, mem_v, l, t_new)
        ys = _mlp(ys, oc.reshape(ns * t_new, -1), *mlp_w, ns * t_new, l == depth - 1)
        outs["ks"].append(k.reshape(ns, t_new, N_HEADS, HEAD_DIM))
        outs["vs"].append(v.reshape(ns, t_new, N_HEADS, HEAD_DIM))
        outs["iks"].append(ik.reshape(ns, t_new, IDX_DIM))
        outs["srs"].append(s_re.reshape(ns, groups, state))
        outs["sis"].append(s_im.reshape(ns, groups, state))

    st = lambda name: jnp.stack(outs[name])
    return (yp.reshape(nb, seq, d), ys.reshape(ns, t_new, d),
            st("kp"), st("vp"), st("ikp"), st("srp"), st("sip"), st("mkp"), st("mvp"),
            st("ks"), st("vs"), st("iks"), st("srs"), st("sis"))
```

```python
import functools
import math

import jax
import jax.numpy as jnp
import numpy as np
from jax import lax
from jax.experimental import pallas as pl
from jax.experimental.pallas import tpu as pltpu

F32 = jnp.float32
BF16 = jnp.bfloat16
I32 = jnp.int32

EPS = 1e-6
N_HEADS = 8
HEAD_DIM = 64
ATTN_WIDTH = N_HEADS * HEAD_DIM
IDX_HEADS = 8
IDX_DIM = 64
TOP_K_MAX = 256
SSM_GROUP = 16
SSM_STATE = 64
X_HEADS = 4
X_HEAD_DIM = 128
N_BUCKETS = 32
MAX_DISTANCE = 128
PAGE_SIZE = 128
V_ROWS = HEAD_DIM + 16

MXU_DEPTH = 256
LANES = 128
SUBLANES = 8
VMEM_LIMIT_BYTES = 56 * 1024 * 1024
INT_MIN = -(2 ** 31)
INT_MAX = 2 ** 31 - 1
EARLY_EXIT_BIT = 26
F32_MIN = float(np.finfo(np.float32).min)
F32_MAX = float(np.finfo(np.float32).max)
M_INIT = -1e30

NT_DIMS = (((1,), (1,)), ((), ()))


def _params(*semantics):
    return pltpu.CompilerParams(dimension_semantics=semantics, vmem_limit_bytes=VMEM_LIMIT_BYTES)


def _rmsnorm(x, g):
    ms = jnp.mean(x * x, axis=-1, keepdims=True)
    return x * lax.rsqrt(ms + EPS) * g


def _dot(a, b):
    return jnp.dot(a, b, preferred_element_type=F32)


def _dot_nt(a, b):
    return lax.dot_general(a, b, NT_DIMS, preferred_element_type=F32)


def _monotone_key(x):
    b = lax.bitcast_convert_type(x, I32)
    key = jnp.where(b < 0, b ^ INT_MAX, b)
    return jnp.where(x == 0.0, 0, key)


def _row_tile(rows, want):
    t = min(rows, want)
    assert rows % t == 0, (rows, t)
    return t


def _t5_bucket(dist):
    max_exact = N_BUCKETS // 2
    d = jnp.maximum(dist, 0)
    df = jnp.maximum(d, 1).astype(F32)
    large = max_exact + (jnp.log(df / max_exact) / math.log(MAX_DISTANCE / max_exact)
                         * (N_BUCKETS - max_exact)).astype(I32)
    large = jnp.minimum(large, N_BUCKETS - 1)
    return jnp.where(d < max_exact, d, large)


def _bias_lookup(rel_ref, bucket, h):
    out = jnp.zeros(bucket.shape, F32)
    for b in range(N_BUCKETS):
        out = jnp.where(bucket == b, rel_ref[b, h], out)
    return out - rel_ref[N_BUCKETS - 1, h]


def _bias_prompt_kernel(rel_ref, o_ref, *, tq):
    r = lax.broadcasted_iota(I32, (2 * tq, tq), 0)
    j = lax.broadcasted_iota(I32, (2 * tq, tq), 1)
    bucket = _t5_bucket(j - r + tq)
    for h in range(N_HEADS):
        o_ref[h] = _bias_lookup(rel_ref, bucket, h)


def _bias_sample_kernel(rel_ref, last_ref, new_ref, *, t_new):
    row = lax.broadcasted_iota(I32, (N_HEADS * t_new, PAGE_SIZE), 0)
    lane = lax.broadcasted_iota(I32, (N_HEADS * t_new, PAGE_SIZE), 1)
    last = jnp.zeros(row.shape, F32)
    new = jnp.zeros(row.shape, F32)
    for h in range(N_HEADS):
        t = row - h * t_new
        in_head = (t >= 0) & (t < t_new)
        last = jnp.where(in_head, _bias_lookup(rel_ref, _t5_bucket(PAGE_SIZE + t - lane), h), last)
        new = jnp.where(in_head, _bias_lookup(rel_ref, _t5_bucket(t - lane), h), new)
    last_ref[...] = last
    new_ref[...] = new


def _bias_tables(rel_bias, tq, t_new):
    smem = pl.BlockSpec(memory_space=pltpu.SMEM)
    prompt = pl.pallas_call(
        functools.partial(_bias_prompt_kernel, tq=tq),
        out_shape=jax.ShapeDtypeStruct((N_HEADS, 2 * tq, tq), F32),
        in_specs=[smem], name="bias_prompt")(rel_bias)
    last, new = pl.pallas_call(
        functools.partial(_bias_sample_kernel, t_new=t_new),
        out_shape=[jax.ShapeDtypeStruct((N_HEADS * t_new, PAGE_SIZE), F32)] * 2,
        in_specs=[smem], name="bias_sample")(rel_bias)
    return prompt, last, new


def _proj_prompt_kernel(x_ref, g_ref, wqT_ref, wiqT_ref, wiwT_ref, wk_ref, wkT_ref, wvT_ref, wik_ref, wikT_ref,
                        wu_ref, wg_ref,
                        qT_ref, iqw_ref, iwT_ref, kT_ref, vT_ref, ikT_ref, kb_ref, vTb_ref, ikb_ref, u_ref, gate_ref,
                        *, tq):
    h = _rmsnorm(x_ref[0], g_ref[...]).astype(BF16)
    qT_ref[0] = _dot_nt(wqT_ref[...], h).astype(BF16)
    iqT = _dot_nt(wiqT_ref[...], h).astype(BF16)
    for blk in range(iqw_ref.shape[1]):
        for hh in range(IDX_HEADS):
            iqw_ref[0, blk, :, hh * tq:(hh + 1) * tq] = iqT[hh * IDX_DIM:(hh + 1) * IDX_DIM, blk * tq:(blk + 1) * tq]
    iwT_ref[0] = _dot_nt(wiwT_ref[...], h) * (IDX_HEADS ** -0.5)
    kT_ref[0] = _dot_nt(wkT_ref[...], h)
    kb_ref[0] = _dot(h, wk_ref[...]).astype(BF16)
    vT = _dot_nt(wvT_ref[...], h)
    vT_ref[0] = vT
    ones = jnp.ones((V_ROWS - HEAD_DIM, vT.shape[1]), BF16)
    for hh in range(N_HEADS):
        vTb_ref[0, hh * V_ROWS:hh * V_ROWS + HEAD_DIM, :] = vT[hh * HEAD_DIM:(hh + 1) * HEAD_DIM, :].astype(BF16)
        vTb_ref[0, hh * V_ROWS + HEAD_DIM:(hh + 1) * V_ROWS, :] = ones
    ikT_ref[0] = _dot_nt(wikT_ref[...], h)
    ikb_ref[0] = _dot(h, wik_ref[...]).astype(BF16)
    u_ref[0] = _dot(h, wu_ref[...])
    gate_ref[0] = jax.nn.sigmoid(_dot(h, wg_ref[...]))


def _proj_rows_kernel(x_ref, g_ref, wq_ref, wiq_ref, wiw_ref, wk_ref, wv_ref, wik_ref, wu_ref, wg_ref,
                      q_ref, iq_ref, iw_ref, k_ref, v_ref, ik_ref, u_ref, gate_ref):
    h = _rmsnorm(x_ref[...], g_ref[...]).astype(BF16)
    q_ref[...] = _dot(h, wq_ref[...])
    iq_ref[...] = _dot(h, wiq_ref[...])
    iw_ref[...] = _dot(h, wiw_ref[...]) * (IDX_HEADS ** -0.5)
    k_ref[...] = _dot(h, wk_ref[...])
    v_ref[...] = _dot(h, wv_ref[...])
    ik_ref[...] = _dot(h, wik_ref[...])
    u_ref[...] = _dot(h, wu_ref[...])
    gate_ref[...] = jax.nn.sigmoid(_dot(h, wg_ref[...]))


def _split_w_in(w_in):
    sizes = (ATTN_WIDTH, ATTN_WIDTH, ATTN_WIDTH, IDX_HEADS * IDX_DIM, IDX_DIM, IDX_HEADS)
    offs = np.cumsum((0,) + sizes)
    d_model = w_in.shape[0]
    ssm_width = (w_in.shape[1] - offs[-1] - 2 * d_model)
    wq, wk, wv, wiq, wik, wiw = (w_in[:, offs[i]:offs[i + 1]] for i in range(6))
    wu = w_in[:, offs[-1]:offs[-1] + ssm_width]
    wg = w_in[:, offs[-1] + ssm_width:]
    wq = wq * (HEAD_DIM ** -0.5)
    wiq = wiq * (IDX_DIM ** -0.5)
    return tuple(w.astype(BF16) for w in (wq, wk, wv, wiq, wik, wiw, wu, wg))


def _full(shape):
    return pl.BlockSpec(shape, lambda *_: (0,) * len(shape), pipeline_mode=pl.Buffered(1))


def _proj_prompt(x, g, ws, tm, tq):
    n, t, d = x.shape
    wq, wk, wv, wiq, wik, wiw, wu, wg = ws
    sw, gw = wu.shape[1], wg.shape[1]
    assert tm % tq == 0
    weights = (wq.T, wiq.T, wiw.T, wk, wk.T, wv.T, wik, wik.T, wu, wg)
    rows = lambda w: pl.BlockSpec((1, tm, w), lambda b, i: (b, i, 0))
    cols = lambda w: pl.BlockSpec((1, w, tm), lambda b, i: (b, 0, i))
    iq_wide = IDX_HEADS * tq
    out_shape = [
        jax.ShapeDtypeStruct((n, ATTN_WIDTH, t), BF16),
        jax.ShapeDtypeStruct((n, t // tq, IDX_DIM, iq_wide), BF16),
        jax.ShapeDtypeStruct((n, IDX_HEADS, t), F32),
        jax.ShapeDtypeStruct((n, ATTN_WIDTH, t), F32),
        jax.ShapeDtypeStruct((n, ATTN_WIDTH, t), F32),
        jax.ShapeDtypeStruct((n, IDX_DIM, t), F32),
        jax.ShapeDtypeStruct((n, t, ATTN_WIDTH), BF16),
        jax.ShapeDtypeStruct((n, N_HEADS * V_ROWS, t), BF16),
        jax.ShapeDtypeStruct((n, t, IDX_DIM), BF16),
        jax.ShapeDtypeStruct((n, t, sw), F32),
        jax.ShapeDtypeStruct((n, t, gw), F32),
    ]
    out_specs = [cols(ATTN_WIDTH), pl.BlockSpec((1, tm // tq, IDX_DIM, iq_wide), lambda b, i: (b, i, 0, 0)),
                 cols(IDX_HEADS), cols(ATTN_WIDTH), cols(ATTN_WIDTH), cols(IDX_DIM), rows(ATTN_WIDTH),
                 cols(N_HEADS * V_ROWS), rows(IDX_DIM), rows(sw), rows(gw)]
    return pl.pallas_call(
        functools.partial(_proj_prompt_kernel, tq=tq), grid=(n, t // tm), out_shape=out_shape,
        in_specs=[rows(d), _full((1, d))] + [_full(w.shape) for w in weights],
        out_specs=out_specs, compiler_params=_params("parallel", "parallel"), name="proj_prompt",
    )(x, g.reshape(1, d), *weights)


def _proj_rows(x, g, ws):
    r, d = x.shape
    wq, wk, wv, wiq, wik, wiw, wu, wg = ws
    weights = (wq, wiq, wiw, wk, wv, wik, wu, wg)
    out_shape = [jax.ShapeDtypeStruct((r, w.shape[1]), F32) for w in weights]
    return pl.pallas_call(
        _proj_rows_kernel, out_shape=out_shape, compiler_params=_params(), name="proj_rows",
    )(x, g.reshape(1, d), *weights)


def _dsa_prompt_kernel(qT_ref, iqw_ref, iwT_ref, ikb_ref, kb_ref, vT_ref, bias_ref, o_ref,
                       s_ref, t_ref, pen_ref, pnear_ref, q2_ref, sbuf_ref, pbuf_ref, acc_ref, m_ref, l_ref,
                       *, tq, top_k):
    i = pl.program_id(1)
    gk = KEY_GROUP * tq
    n_grp = lax.div(i + KEY_GROUP, KEY_GROUP)
    n_far_grp = lax.div(jnp.maximum(i - 1, 0) + KEY_GROUP - 1, KEY_GROUP)
    iw = iwT_ref[0]
    row = lax.broadcasted_iota(I32, (tq, tq), 0)
    col = lax.broadcasted_iota(I32, (tq, tq), 1)
    grow = lax.broadcasted_iota(I32, (gk, tq), 0)

    ahead = row - col

    def score_group(g, has_future):
        for cc in range(KEY_GROUP):
            c = g * KEY_GROUP + cc
            r0 = pl.multiple_of(c * tq, tq)
            d = _dot(ikb_ref[0, pl.ds(r0, tq), :], iqw_ref[0, 0])
            sc = jnp.zeros((tq, tq), F32)
            for h in range(IDX_HEADS):
                sc = sc + iw[h:h + 1, :] * jnp.maximum(d[:, h * tq:(h + 1) * tq], 0.0)
            key = _monotone_key(sc)
            if has_future:
                key = jnp.where(ahead > (i - c) * tq, INT_MIN, key)
            s_ref[pl.ds(r0, tq), :] = key

    def past_group(g, carry):
        score_group(g, False)
        return carry

    lax.fori_loop(0, n_grp - 1, past_group, 0)
    score_group(n_grp - 1, True)

    def count(pred):
        def body(g, acc8):
            r0 = pl.multiple_of(g * gk, gk)
            ind = jnp.where(pred(s_ref[pl.ds(r0, gk), :], r0 + grow), 1, 0)
            return acc8 + ind.reshape(gk // SUBLANES, SUBLANES, tq).sum(axis=0)
        acc8 = lax.fori_loop(0, n_grp, body, jnp.zeros((SUBLANES, tq), I32))
        return acc8.sum(axis=0, keepdims=True)

    def bit_step(it, state):
        tu, cnt_ge = state
        cand_u = tu | lax.shift_left(jnp.int32(1), 31 - it)
        cand = cand_u ^ INT_MIN
        cnt = count(lambda blk, r0: blk >= cand)
        ok = cnt >= top_k
        return jnp.where(ok, cand_u, tu), jnp.where(ok, cnt, cnt_ge)

    state = lax.fori_loop(0, EARLY_EXIT_BIT, bit_step, (jnp.zeros((1, tq), I32), jnp.zeros((1, tq), I32)))
    t_ref[0:1, :], t_ref[1:2, :] = state
    unsettled = (state[1] != top_k) & (state[1] != 0)

    @pl.when(jnp.max(unsettled.astype(I32)) > 0)
    def _():
        t_ref[0:1, :], t_ref[1:2, :] = lax.fori_loop(EARLY_EXIT_BIT, 32, bit_step, state)

    tu, cnt_ge = t_ref[0:1, :], t_ref[1:2, :]
    thr = jnp.maximum(tu ^ INT_MIN, INT_MIN + 1)
    has_ties = jnp.max((cnt_ge > top_k).astype(I32)) > 0

    w_chunk = jnp.maximum(i - 1, 0)
    w0 = pl.multiple_of(w_chunk * tq, tq)

    @pl.when(jnp.logical_not(has_ties))
    def _():
        def pen_group(g, carry):
            r0 = pl.multiple_of(g * gk, gk)
            sel = (s_ref[pl.ds(r0, gk), :] >= thr) & (r0 + grow < (i - 1) * tq)
            pen_ref[pl.ds(r0, gk), :] = jnp.where(sel, 0.0, F32_MIN)
            return carry

        lax.fori_loop(0, jnp.maximum(n_far_grp, 1), pen_group, 0)
        pnear_ref[...] = jnp.where(s_ref[pl.ds(w0, 2 * tq), :] >= thr, 0.0, F32_MIN)

    @pl.when(has_ties)
    def _():
        need = (top_k - count(lambda blk, r0: blk > thr)).astype(F32)
        lower = jnp.where(row >= col, 1.0, 0.0).astype(BF16)
        def select_chunk(r0, seen):
            blk = s_ref[pl.ds(r0, tq), :]
            tie = blk == thr
            tie01 = jnp.where(tie, 1.0, 0.0)
            rank = _dot(lower, tie01.astype(BF16)) + seen
            seen = seen + tie01.reshape(tq // SUBLANES, SUBLANES, tq).sum(axis=0).sum(axis=0, keepdims=True)
            return (blk > thr) | (tie & (rank <= need)), seen

        def pen_group(g, carry):
            seen, seen_window = carry
            for cc in range(KEY_GROUP):
                c = g * KEY_GROUP + cc
                r0 = pl.multiple_of(c * tq, tq)
                seen_window = jnp.where(c == w_chunk, seen, seen_window)
                sel, seen = select_chunk(r0, seen)
                pen_ref[pl.ds(r0, tq), :] = jnp.where(sel & (c < i - 1), 0.0, F32_MIN)
            return seen, seen_window

        zeros = jnp.zeros((1, tq), F32)
        _, seen = lax.fori_loop(0, n_grp, pen_group, (zeros, zeros))
        for k in range(2):
            sel, seen = select_chunk(w0 + k * tq, seen)
            pnear_ref[k * tq:(k + 1) * tq, :] = jnp.where(sel, 0.0, F32_MIN)

    zero = jnp.zeros((HEAD_DIM, tq), BF16)
    for hp in range(N_HEADS // 2):
        lo = qT_ref[0, (2 * hp) * HEAD_DIM:(2 * hp + 1) * HEAD_DIM, :]
        hi = qT_ref[0, (2 * hp + 1) * HEAD_DIM:(2 * hp + 2) * HEAD_DIM, :]
        q2_ref[hp * 2 * HEAD_DIM:(hp + 1) * 2 * HEAD_DIM, :] = jnp.concatenate(
            [jnp.concatenate([lo, zero], axis=1), jnp.concatenate([zero, hi], axis=1)], axis=0)
    m_ref[...] = jnp.full(m_ref.shape, M_INIT, F32)
    l_ref[...] = jnp.zeros(l_ref.shape, F32)
    acc_ref[...] = jnp.zeros(acc_ref.shape, F32)

    def update_head(h, m8, r0, n_c):
        hs = slice(h * HEAD_DIM, (h + 1) * HEAD_DIM)
        ls = slice(h * SUBLANES, (h + 1) * SUBLANES)
        m_old = m_ref[h:h + 1, :]
        m_new = jnp.maximum(m_old, m8.max(axis=0, keepdims=True))
        alpha = jnp.exp(m_old - m_new)
        for cc in range(n_c):
            p = jnp.exp(sbuf_ref[h, cc * tq:(cc + 1) * tq, :] - m_new)
            pbuf_ref[h, cc * tq:(cc + 1) * tq, :] = p.astype(BF16)
        pv = _dot(vT_ref[0, h * V_ROWS:(h + 1) * V_ROWS, pl.ds(r0, n_c * tq)], pbuf_ref[h, 0:n_c * tq, :])
        acc_ref[hs, :] = alpha * acc_ref[hs, :] + pv[0:HEAD_DIM, :]
        l_ref[ls, :] = alpha * l_ref[ls, :] + pv[HEAD_DIM:HEAD_DIM + SUBLANES, :]
        m_ref[h:h + 1, :] = m_new

    def logits(hp, r0, n_c, pen_src, pen_r0, bias_rows):
        m8 = [None, None]
        for cc in range(n_c):
            pen = pen_src[pl.ds(pen_r0 + cc * tq, tq), :]
            kslab = kb_ref[0, pl.ds(r0 + cc * tq, tq), hp * 2 * HEAD_DIM:(hp + 1) * 2 * HEAD_DIM]
            s2 = _dot(kslab, q2_ref[hp * 2 * HEAD_DIM:(hp + 1) * 2 * HEAD_DIM, :])
            for e in range(2):
                h = 2 * hp + e
                s = s2[:, e * tq:(e + 1) * tq] + pen
                if bias_rows is not None:
                    s = s + bias_ref[h, pl.ds(bias_rows[cc], tq), :]
                sbuf_ref[h, cc * tq:(cc + 1) * tq, :] = s
                cm = s.reshape(tq // SUBLANES, SUBLANES, tq).max(axis=0)
                m8[e] = cm if m8[e] is None else jnp.maximum(m8[e], cm)
        return tuple(m8)

    def attend_group(m8, r0, n_c, pen_src, pen_r0, bias_rows, next_group_logits):
        pairs = N_HEADS // 2
        for hp in range(pairs):
            if hp + 1 < pairs:
                m8_next = logits(hp + 1, r0, n_c, pen_src, pen_r0, bias_rows)
            else:
                m8_next = next_group_logits() if next_group_logits is not None else None
            for e in range(2):
                update_head(2 * hp + e, m8[e], r0, n_c)
            m8 = m8_next
        return m8

    def far_logits0(g):
        r0 = pl.multiple_of(g * gk, gk)
        return logits(0, r0, KEY_GROUP, pen_ref, r0, None)

    def far_group(g, m8):
        r0 = pl.multiple_of(g * gk, gk)
        nxt = jnp.minimum(g + 1, n_far_grp - 1)
        return attend_group(m8, r0, KEY_GROUP, pen_ref, r0, None, lambda: far_logits0(nxt))

    lax.fori_loop(0, n_far_grp, far_group, far_logits0(0))
    first_bias = pl.multiple_of(jnp.where(i == 0, tq, 0), tq)
    near = (w0, 2, pnear_ref, 0, (first_bias, tq))
    attend_group(logits(0, *near), *near, None)

    for h in range(N_HEADS):
        hs = slice(h * HEAD_DIM, (h + 1) * HEAD_DIM)
        acc_ref[hs, :] = acc_ref[hs, :] / l_ref[h * SUBLANES:h * SUBLANES + 1, :]
    o_ref[0] = acc_ref[...].T.astype(BF16)


def _dsa_prompt(qT, iqw, iwT, ikb, kb, vT, bias, tq):
    n, _, t = qT.shape
    n_blk = t // tq
    assert t % tq == 0 and tq >= MAX_DISTANCE and n_blk % KEY_GROUP == 0 and n_blk >= 2, (t, tq)
    top_k = min(TOP_K_MAX, t // 4)
    gk = KEY_GROUP * tq
    colblk = lambda w: pl.BlockSpec((1, w, tq), lambda b, i: (b, 0, i))
    whole = lambda a: pl.BlockSpec((1,) + a.shape[1:], lambda b, i: (b, 0, 0), pipeline_mode=pl.Buffered(1))
    kern = functools.partial(_dsa_prompt_kernel, tq=tq, top_k=top_k)
    return pl.pallas_call(
        kern, grid=(n, n_blk),
        out_shape=jax.ShapeDtypeStruct((n, t, ATTN_WIDTH), BF16),
        in_specs=[colblk(ATTN_WIDTH), pl.BlockSpec((1, 1) + iqw.shape[2:], lambda b, i: (b, i, 0, 0)),
                  colblk(IDX_HEADS), whole(ikb), whole(kb), whole(vT),
                  pl.BlockSpec(bias.shape, lambda b, i: (0, 0, 0), pipeline_mode=pl.Buffered(1))],
        out_specs=pl.BlockSpec((1, tq, ATTN_WIDTH), lambda b, i: (b, i, 0)),
        scratch_shapes=[pltpu.VMEM((t, tq), I32),
                        pltpu.VMEM((SUBLANES, tq), I32),
                        pltpu.VMEM((t, tq), F32),
                        pltpu.VMEM((2 * tq, tq), F32),
                        pltpu.VMEM((N_HEADS * HEAD_DIM, 2 * tq), BF16),
                        pltpu.VMEM((N_HEADS, gk, tq), F32),
                        pltpu.VMEM((N_HEADS, gk, tq), BF16),
                        pltpu.VMEM((ATTN_WIDTH, tq), F32),
                        pltpu.VMEM((N_HEADS, tq), F32),
                        pltpu.VMEM((N_HEADS * SUBLANES, tq), F32)],
        compiler_params=_params("parallel", "arbitrary"), name="dsa_prompt",
    )(qT, iqw, iwT, ikb, kb, vT, bias)


def _page_specs(page_shape, layer, pages_per_step):
    def spec(r):
        return pl.BlockSpec((1, 1) + page_shape,
                            lambda b, g, pt: (layer, pt[b, g * pages_per_step + r]) + (0,) * len(page_shape))
    return [spec(r) for r in range(pages_per_step)]


def _head_sum(x, t_new):
    return x.reshape(IDX_HEADS, t_new, x.shape[-1]).sum(axis=0)


def _dsa_sample_score_kernel(pt_ref, *refs, pps, t_new):
    page_refs = refs[:pps]
    iq_ref, w_ref, iknew_ref, sp_ref, sn_ref = refs[pps:]
    g = pl.program_id(1)
    iq = iq_ref[0]
    w = w_ref[0]
    for r in range(pps):
        d = _dot(iq, page_refs[r][0, 0].astype(BF16))
        sc = _head_sum(w * jnp.maximum(d, 0.0), t_new)
        sp_ref[0, :, r * PAGE_SIZE:(r + 1) * PAGE_SIZE] = _monotone_key(sc)

    @pl.when(g == pl.num_programs(1) - 1)
    def _():
        d = _dot_nt(iq, iknew_ref[0].astype(BF16))
        sc = _head_sum(w * jnp.maximum(d, 0.0), t_new)
        t = lax.broadcasted_iota(I32, sc.shape, 0)
        j = lax.broadcasted_iota(I32, sc.shape, 1)
        sn_ref[0] = jnp.where(j > t, INT_MIN, _monotone_key(sc))


def _dsa_sample_thr_kernel(sp_ref, sn_ref, thr_ref, j_ref, *, top_k, idx_bits):
    rows, past = sp_ref.shape
    lane = lax.broadcasted_iota(I32, (rows, LANES), 1)

    def count(pred):
        tot = jnp.where(pred(sn_ref[...], past + lane), 1, 0)
        for c in range(past // LANES):
            tot = tot + jnp.where(pred(sp_ref[:, c * LANES:(c + 1) * LANES], c * LANES + lane), 1, 0)
        return jnp.broadcast_to(tot.sum(axis=-1, keepdims=True), (rows, LANES))

    def bit_step(it, tu):
        cand_u = tu | lax.shift_left(jnp.int32(1), 31 - it)
        cand = cand_u ^ INT_MIN
        cnt = count(lambda blk, idx: blk >= cand)
        return jnp.where(cnt >= top_k, cand_u, tu)

    tu = lax.fori_loop(0, 32, bit_step, jnp.zeros((rows, LANES), I32))
    thr = jnp.maximum(tu ^ INT_MIN, INT_MIN + 1)
    cnt_gt = count(lambda blk, idx: blk > thr)
    cnt_ge = count(lambda blk, idx: blk >= thr)
    need = top_k - cnt_gt
    multi = cnt_ge > top_k
    thr_ref[...] = thr
    j_ref[...] = jnp.full((rows, LANES), INT_MAX, I32)

    @pl.when(jnp.max(multi.astype(I32)) > 0)
    def _():
        def j_step(it, jv):
            cand = jv | lax.shift_left(jnp.int32(1), idx_bits - 1 - it)
            cnt = count(lambda blk, idx: (blk == thr) & (idx < cand))
            return jnp.where(cnt < need, cand, jv)
        jv = lax.fori_loop(0, idx_bits, j_step, jnp.zeros((rows, LANES), I32))
        j_ref[...] = jnp.where(multi, jv, INT_MAX)


def _dsa_sample_attn_kernel(pt_ref, *refs, pps, t_new):
    k_refs = refs[:pps]
    v_refs = refs[pps:2 * pps]
    (q_ref, sp_ref, sn_ref, thr_ref, j_ref, knew_ref, vnew_ref, blast_ref, bnew_ref,
     o_ref, acc_ref, m_ref, l_ref) = refs[2 * pps:]
    g = pl.program_id(1)
    last = g == pl.num_programs(1) - 1
    thr = thr_ref[0]
    jsel = j_ref[0]
    lane = lax.broadcasted_iota(I32, (t_new, PAGE_SIZE), 1)

    @pl.when(g == 0)
    def _():
        m_ref[...] = jnp.full(m_ref.shape, M_INIT, F32)
        l_ref[...] = jnp.zeros(l_ref.shape, F32)
        acc_ref[...] = jnp.zeros(acc_ref.shape, F32)

    lane_tiles = ATTN_WIDTH // LANES
    q = q_ref[0]

    def update(pages):
        logit = []
        for keys, idx, kT, _, bias in pages:
            s = _dot(q, kT.astype(BF16))
            if bias is not None:
                s = s + bias
            sel = (keys > thr) | ((keys == thr) & (idx <= jsel))
            cap = jnp.where(sel, F32_MAX, F32_MIN)
            logit.append(jnp.minimum(s, jnp.concatenate([cap] * N_HEADS, axis=0)))
        m_old = m_ref[...]
        m_new = jnp.maximum(m_old, jnp.broadcast_to(functools.reduce(jnp.maximum, logit).max(axis=-1, keepdims=True),
                                                    m_old.shape))
        prob = [jnp.exp(s - m_new) for s in logit]
        alpha = jnp.exp(m_old - m_new)
        l_ref[...] = alpha * l_ref[...] + jnp.broadcast_to(sum(prob).sum(axis=-1, keepdims=True), m_old.shape)
        pv = sum(_dot_nt(p.astype(BF16), page[3].astype(BF16)) for p, page in zip(prob, pages))
        acc_ref[...] = jnp.concatenate([alpha] * lane_tiles, axis=1) * acc_ref[...] + pv
        m_ref[...] = m_new

    last_f = jnp.where(last, 1.0, 0.0)
    update([(sp_ref[0, :, r * PAGE_SIZE:(r + 1) * PAGE_SIZE], (g * pps + r) * PAGE_SIZE + lane,
             k_refs[r][0, 0], v_refs[r][0, 0],
             blast_ref[...] * last_f if r == pps - 1 else None)
            for r in range(pps)])

    @pl.when(last)
    def _():
        n_past = pl.num_programs(1) * pps * PAGE_SIZE
        update([(sn_ref[0], n_past + lane, knew_ref[0], vnew_ref[0], bnew_ref[...])])
        o_full = acc_ref[...] / jnp.concatenate([l_ref[...]] * lane_tiles, axis=1)
        out_lane = lax.broadcasted_iota(I32, (t_new, ATTN_WIDTH), 1)
        out = jnp.zeros((t_new, ATTN_WIDTH), F32)
        for h in range(N_HEADS):
            own = (out_lane >= h * HEAD_DIM) & (out_lane < (h + 1) * HEAD_DIM)
            out = jnp.where(own, o_full[h * t_new:(h + 1) * t_new, :], out)
        o_ref[0] = out.astype(BF16)


def _dsa_sample(q, iq, iw, k_new, v_new, ik_new, pool_kT, pool_vT, pool_ikT, layer, page_table, bias_last,
                bias_new):
    n, t_new, _ = q.shape
    n_pages = page_table.shape[1]
    past = n_pages * PAGE_SIZE
    top_k = min(TOP_K_MAX, (past + t_new) // 4)
    ht = N_HEADS * t_new
    assert t_new <= PAGE_SIZE and IDX_HEADS == N_HEADS

    heads = lambda a, dim: a.reshape(n, t_new, N_HEADS, dim).transpose(0, 2, 1, 3)
    iq_hm = heads(iq, IDX_DIM).reshape(n, ht, IDX_DIM).astype(BF16)
    w_hm = jnp.broadcast_to(iw.transpose(0, 2, 1).reshape(n, ht, 1), (n, ht, LANES))
    eye = jnp.eye(N_HEADS, dtype=F32)
    q_bd = (heads(q, HEAD_DIM)[:, :, :, None, :] * eye[None, :, None, :, None]
            ).reshape(n, ht, ATTN_WIDTH).astype(BF16)
    page = lambda a: jnp.pad(heads(a, HEAD_DIM).transpose(0, 1, 3, 2).reshape(n, ATTN_WIDTH, t_new),
                             ((0, 0), (0, 0), (0, PAGE_SIZE - t_new)))
    k_page, v_page = page(k_new), page(v_new)
    ik_pad = jnp.pad(ik_new, ((0, 0), (0, PAGE_SIZE - t_new), (0, 0)))

    per_b = lambda a: pl.BlockSpec((1,) + a.shape[1:], lambda b, g, pt: (b,) + (0,) * (a.ndim - 1))
    const = lambda a: pl.BlockSpec(a.shape, lambda b, g, pt: (0,) * a.ndim)
    sp_spec = lambda pages: pl.BlockSpec((1, t_new, pages * PAGE_SIZE), lambda b, g, pt: (b, 0, g))

    pps = math.gcd(n_pages, SCORE_PAGES_PER_STEP)
    sp, sn = pl.pallas_call(
        functools.partial(_dsa_sample_score_kernel, pps=pps, t_new=t_new),
        grid_spec=pltpu.PrefetchScalarGridSpec(
            num_scalar_prefetch=1, grid=(n, n_pages // pps),
            in_specs=_page_specs((IDX_DIM, PAGE_SIZE), layer, pps) + [per_b(iq_hm), per_b(w_hm), per_b(ik_pad)],
            out_specs=[sp_spec(pps), pl.BlockSpec((1, t_new, PAGE_SIZE), lambda b, g, pt: (b, 0, 0))]),
        out_shape=[jax.ShapeDtypeStruct((n, t_new, past), I32), jax.ShapeDtypeStruct((n, t_new, PAGE_SIZE), I32)],
        compiler_params=_params("parallel", "arbitrary"), name="dsa_sample_score",
    )(page_table, *([pool_ikT] * pps), iq_hm, w_hm, ik_pad)

    rows = n * t_new
    rt = _row_tile(rows, THR_ROW_TILE)
    idx_bits = int(math.ceil(math.log2(past + PAGE_SIZE)))
    thr, jsel = pl.pallas_call(
        functools.partial(_dsa_sample_thr_kernel, top_k=top_k, idx_bits=idx_bits),
        grid=(rows // rt,),
        in_specs=[pl.BlockSpec((rt, past), lambda i: (i, 0)), pl.BlockSpec((rt, PAGE_SIZE), lambda i: (i, 0))],
        out_specs=[pl.BlockSpec((rt, LANES), lambda i: (i, 0))] * 2,
        out_shape=[jax.ShapeDtypeStruct((rows, LANES), I32)] * 2,
        compiler_params=_params("parallel"), name="dsa_sample_thr",
    )(sp.reshape(rows, past), sn.reshape(rows, PAGE_SIZE))
    thr = thr.reshape(n, t_new, LANES)
    jsel = jsel.reshape(n, t_new, LANES)

    kv_page = (ATTN_WIDTH, PAGE_SIZE)
    pps = math.gcd(n_pages, ATTN_PAGES_PER_STEP)
    return pl.pallas_call(
        functools.partial(_dsa_sample_attn_kernel, pps=pps, t_new=t_new),
        grid_spec=pltpu.PrefetchScalarGridSpec(
            num_scalar_prefetch=1, grid=(n, n_pages // pps),
            in_specs=(_page_specs(kv_page, layer, pps) + _page_specs(kv_page, layer, pps)
                      + [per_b(q_bd), sp_spec(pps), per_b(sn), per_b(thr), per_b(jsel), per_b(k_page), per_b(v_page),
                         const(bias_last), const(bias_new)]),
            out_specs=pl.BlockSpec((1, t_new, ATTN_WIDTH), lambda b, g, pt: (b, 0, 0)),
            scratch_shapes=[pltpu.VMEM((ht, ATTN_WIDTH), F32), pltpu.VMEM((ht, LANES), F32),
                            pltpu.VMEM((ht, LANES), F32)]),
        out_shape=jax.ShapeDtypeStruct((n, t_new, ATTN_WIDTH), BF16),
        compiler_params=_params("parallel", "arbitrary"), name="dsa_sample_attn",
    )(page_table, *([pool_kT] * pps), *([pool_vT] * pps), q_bd, sp, sn, thr, jsel, k_page, v_page, bias_last,
      bias_new)


def _gelu_tanh(y):
    return 0.5 * y * (1.0 + jnp.tanh(math.sqrt(2.0 / math.pi) * (y + 0.044715 * (y * y * y))))


def _ssm_kernel(u_ref, x0r_ref, x0i_ref, lr_ref, li_ref, b_ref, c_ref, d_ref, wglu_ref,
                y_ref, sr_ref, si_ref, x_ref, st_ref, *, tile):
    j = pl.program_id(1)
    ns = lr_ref.shape[1]
    seqs = range(u_ref.shape[0])

    @pl.when(j == 0)
    def _():
        for b in seqs:
            st_ref[b, 0:1, :] = x0r_ref[b]
            st_ref[b, 1:2, :] = x0i_ref[b]

    width = u_ref.shape[2]
    clusters = max(1, width // MXU_DEPTH)
    cw = width // clusters
    sw = ns // clusters
    for b in seqs:
        ub = u_ref[b].astype(BF16)
        for k in range(clusters):
            for part in range(2):
                cols = slice(part * ns + k * sw, part * ns + (k + 1) * sw)
                x_ref[b, :, cols] = _dot(ub[:, k * cw:(k + 1) * cw], b_ref[k * cw:(k + 1) * cw, cols])
    lr = lr_ref[...]
    li = li_ref[...]

    def step(t, carry):
        out = []
        for b in seqs:
            sr, si = carry[b]
            br = x_ref[b, pl.ds(t, 1), 0:ns]
            bi = x_ref[b, pl.ds(t, 1), ns:2 * ns]
            nr = lr * sr - li * si + br
            ni = lr * si + li * sr + bi
            x_ref[b, pl.ds(t, 1), 0:ns] = nr
            x_ref[b, pl.ds(t, 1), ns:2 * ns] = ni
            out.append((nr, ni))
        return tuple(out)

    final = lax.fori_loop(0, tile, step, tuple((st_ref[b, 0:1, :], st_ref[b, 1:2, :]) for b in seqs))
    for b in seqs:
        st_ref[b, 0:1, :] = final[b][0]
        st_ref[b, 1:2, :] = final[b][1]

    for b in seqs:
        cx = []
        for k in range(clusters):
            ch = slice(k * cw, (k + 1) * cw)
            re = slice(k * sw, (k + 1) * sw)
            im = slice(ns + k * sw, ns + (k + 1) * sw)
            cx.append(_dot(x_ref[b, :, re].astype(BF16), c_ref[re, ch])
                      + _dot(x_ref[b, :, im].astype(BF16), c_ref[im, ch]))
        y = jnp.concatenate(cx, axis=1) + d_ref[...] * u_ref[b]
        y = _gelu_tanh(y)
        y_ref[b] = (y * jax.nn.sigmoid(_dot(y.astype(BF16), wglu_ref[...]))).astype(BF16)

    @pl.when(j == pl.num_programs(1) - 1)
    def _():
        for b in seqs:
            sr_ref[b] = final[b][0]
            si_ref[b] = final[b][1]


def _ssm_params(a_re, a_im, b_re, b_im, c_re, c_im, log_dt):
    g = a_re.shape[0]
    dt = jnp.exp(log_dt)[:, None]
    mag = jnp.exp(a_re * dt)
    lam_re, lam_im = mag * jnp.cos(a_im * dt), mag * jnp.sin(a_im * dt)
    den = a_re * a_re + a_im * a_im
    nr, ni = lam_re - 1.0, lam_im
    f_re = (nr * a_re + ni * a_im) / den
    f_im = (ni * a_re - nr * a_im) / den
    bb_re = f_re[..., None] * b_re - f_im[..., None] * b_im
    bb_im = f_re[..., None] * b_im + f_im[..., None] * b_re
    eye = jnp.eye(g, dtype=F32)
    p, c = bb_re.shape[1], bb_re.shape[2]
    blk_in = lambda m: jnp.einsum('gpc,gh->gchp', m, eye).reshape(g * c, g * p)
    blk_out = lambda m: jnp.einsum('gcp,gh->gphc', m, eye).reshape(g * p, g * c)
    b_blk = jnp.concatenate([blk_in(bb_re), blk_in(bb_im)], axis=1).astype(BF16)
    c_blk = jnp.concatenate([blk_out(c_re), -blk_out(c_im)], axis=0).astype(BF16)
    return lam_re.reshape(1, g * p), lam_im.reshape(1, g * p), b_blk, c_blk


def _ssm(u, x0_re, x0_im, params, d_skip, w_glu, tile):
    n, t, w = u.shape
    lam_re, lam_im, b_blk, c_blk = params
    ns = lam_re.shape[1]
    x0_re = x0_re.reshape(n, 1, ns)
    x0_im = x0_im.reshape(n, 1, ns)
    nseq = math.gcd(n, SSM_SEQS_PER_STEP)
    seq = lambda width: pl.BlockSpec((nseq, tile, width), lambda b, j: (b, j, 0))
    state = pl.BlockSpec((nseq, 1, ns), lambda b, j: (b, 0, 0))
    y, sr, si = pl.pallas_call(
        functools.partial(_ssm_kernel, tile=tile), grid=(n // nseq, t // tile),
        in_specs=[seq(w), state, state, _full((1, ns)), _full((1, ns)), _full(b_blk.shape), _full(c_blk.shape),
                  _full((1, w)), _full(w_glu.shape)],
        out_specs=[seq(w), state, state],
        out_shape=[jax.ShapeDtypeStruct((n, t, w), BF16), jax.ShapeDtypeStruct((n, 1, ns), F32),
                   jax.ShapeDtypeStruct((n, 1, ns), F32)],
        scratch_shapes=[pltpu.VMEM((nseq, tile, 2 * ns), F32), pltpu.VMEM((nseq, SUBLANES, ns), F32)],
        compiler_params=_params("parallel", "arbitrary"), name="ssm",
    )(u, x0_re, x0_im, lam_re, lam_im, b_blk, c_blk, d_skip.reshape(1, w), w_glu)
    return y, sr, si


def _merge_kernel(x_ref, attn_ref, ssm_ref, gate_ref, wua_ref, wus_ref, wo_ref, gx_ref, wxq_ref, xo_ref, qc_ref):
    d = x_ref.shape[-1]
    gate = gate_ref[...]
    mixed = gate[:, 0:d] * _dot(attn_ref[...], wua_ref[...]) + gate[:, d:2 * d] * _dot(ssm_ref[...], wus_ref[...])
    x = x_ref[...] + _dot(mixed.astype(BF16), wo_ref[...])
    xo_ref[...] = x
    qc_ref[...] = _dot(_rmsnorm(x, gx_ref[...]).astype(BF16), wxq_ref[...]).astype(BF16)


def _merge(x, attn, ssm, gates, w_up_attn, w_up_ssm, w_out, g_cross, w_xq, tm):
    r, d = x.shape
    rows = lambda w: pl.BlockSpec((tm, w), lambda i: (i, 0))
    weights = (w_up_attn, w_up_ssm, w_out, g_cross.reshape(1, d), w_xq)
    return pl.pallas_call(
        _merge_kernel, grid=(r // tm,),
        in_specs=[rows(d), rows(attn.shape[1]), rows(ssm.shape[1]), rows(gates.shape[1])]
        + [_full(w.shape) for w in weights],
        out_specs=[rows(d), rows(w_xq.shape[1])],
        out_shape=[jax.ShapeDtypeStruct((r, d), F32), jax.ShapeDtypeStruct((r, w_xq.shape[1]), BF16)],
        compiler_params=_params("parallel"), name="merge",
    )(x, attn, ssm, gates, *weights)


def _cross_kernel(q_ref, mk_ref, mv_ref, o_ref):
    q = q_ref[0]
    outs = []
    for h in range(X_HEADS):
        sl = slice(h * X_HEAD_DIM, (h + 1) * X_HEAD_DIM)
        head_rows = pl.ds(h, mk_ref.shape[2] // X_HEADS, stride=X_HEADS)
        s = _dot_nt(q[:, sl], mk_ref[0, 0, head_rows, :].astype(BF16)) * (X_HEAD_DIM ** -0.5)
        s = s - s.max(axis=-1, keepdims=True)
        p = jnp.exp(s)
        p = (p / p.sum(axis=-1, keepdims=True)).astype(BF16)
        outs.append(_dot(p, mv_ref[0, 0, head_rows, :].astype(BF16)))
    o_ref[0] = jnp.concatenate(outs, axis=-1).astype(BF16)


def _cross(q, mk, mv, layer, tq):
    n, t, w = q.shape
    mem = pl.BlockSpec((1, 1) + mk.shape[2:], lambda b, i: (layer, b, 0, 0))
    blk = pl.BlockSpec((1, tq, w), lambda b, i: (b, i, 0))
    return pl.pallas_call(
        _cross_kernel, grid=(n, t // tq), in_specs=[blk, mem, mem], out_specs=blk,
        out_shape=jax.ShapeDtypeStruct((n, t, w), BF16),
        compiler_params=_params("parallel", "parallel"), name="cross",
    )(q, mk, mv)


def _mlp_kernel(x_ref, oc_ref, wxo_ref, gm_ref, wup_ref, wdn_ref, gf_ref, o_ref, *, ff_chunk, final):
    x = x_ref[...] + _dot(oc_ref[...], wxo_ref[...])
    h = _rmsnorm(x, gm_ref[...]).astype(BF16)
    acc = jnp.zeros(x.shape, F32)
    for c in range(wup_ref.shape[1] // ff_chunk):
        sl = slice(c * ff_chunk, (c + 1) * ff_chunk)
        a = jnp.maximum(_dot(h, wup_ref[:, sl]), 0.0)
        acc = acc + _dot((a * a).astype(BF16), wdn_ref[sl, :])
    x = x + acc
    o_ref[...] = _rmsnorm(x, gf_ref[...]) if final else x


def _mlp(x, o_cross, w_xo, g_mlp, w_up, w_down, g_final, tm, final):
    r, d = x.shape
    rows = lambda w: pl.BlockSpec((tm, w), lambda i: (i, 0))
    weights = (w_xo, g_mlp.reshape(1, d), w_up, w_down, g_final.reshape(1, d))
    return pl.pallas_call(
        functools.partial(_mlp_kernel, ff_chunk=min(1024, w_up.shape[1]), final=final), grid=(r // tm,),
        in_specs=[rows(d), rows(o_cross.shape[1])] + [_full(w.shape) for w in weights],
        out_specs=rows(d), out_shape=jax.ShapeDtypeStruct((r, d), F32),
        compiler_params=_params("parallel"), name="mlp",
    )(x, o_cross, *weights)


def _memkv_kernel(mem_ref, g_ref, wk_ref, wv_ref, mk_ref, mv_ref):
    hm = _rmsnorm(mem_ref[0], g_ref[...]).astype(BF16)
    mk = _dot(hm, wk_ref[...])
    mv = _dot(hm, wv_ref[...])
    for h in range(X_HEADS):
        head_rows = pl.ds(h, hm.shape[0], stride=X_HEADS)
        mk_ref[0, head_rows, :] = mk[:, h * X_HEAD_DIM:(h + 1) * X_HEAD_DIM]
        mv_ref[0, head_rows, :] = mv[:, h * X_HEAD_DIM:(h + 1) * X_HEAD_DIM]


def _memkv(mem, g, w_k, w_v):
    n, m, d = mem.shape
    w = w_k.shape[1]
    assert w == X_HEADS * X_HEAD_DIM
    out_blk = pl.BlockSpec((1, m * X_HEADS, X_HEAD_DIM), lambda b: (b, 0, 0))
    return pl.pallas_call(
        _memkv_kernel, grid=(n,),
        in_specs=[pl.BlockSpec((1, m, d), lambda b: (b, 0, 0)), _full((1, d)), _full(w_k.shape), _full(w_v.shape)],
        out_specs=[out_blk, out_blk], out_shape=[jax.ShapeDtypeStruct((n, m * X_HEADS, X_HEAD_DIM), F32)] * 2,
        compiler_params=_params("parallel"), name="memkv",
    )(mem, g.reshape(1, d), w_k, w_v)


PROMPT_ROW_TILE = 512
PROMPT_Q_TILE = 128
KEY_GROUP = 4
SSM_TIME_TILE = 256
SSM_SEQS_PER_STEP = 8
CROSS_Q_TILE = 512
THR_ROW_TILE = 128
SCORE_PAGES_PER_STEP = 32
ATTN_PAGES_PER_STEP = 16


def kernel(x_prompt, x_sample, mem_prompt, cache_k, cache_v, cache_idx_k, state_ssm_re, state_ssm_im,
           cache_mem_k, cache_mem_v, page_table, rel_bias, norm_mix, w_in, ssm_a_re, ssm_a_im,
           ssm_b_re, ssm_b_im, ssm_c_re, ssm_c_im, ssm_d, ssm_log_dt, w_glu, w_up_attn, w_up_ssm, w_out,
           norm_cross, norm_mem, w_xq, w_xk, w_xv, w_xo, norm_mlp, w_mlp_up, w_mlp_down, norm_final):
    depth = w_in.shape[0]
    nb, seq, d = x_prompt.shape
    ns, t_new, _ = x_sample.shape
    groups, state = ssm_a_re.shape[1], ssm_a_re.shape[2]
    n_mem = mem_prompt.shape[1]
    tm = _row_tile(seq, PROMPT_ROW_TILE)
    tq = _row_tile(seq, PROMPT_Q_TILE)
    ts = _row_tile(seq, SSM_TIME_TILE)
    bf = lambda a: a.astype(BF16)

    bias_prompt, bias_last, bias_new = _bias_tables(rel_bias, tq, t_new)
    yp = x_prompt.reshape(nb * seq, d)
    ys = x_sample.reshape(ns * t_new, d)
    zero_state = jnp.zeros((nb, groups * state), F32)
    outs = {name: [] for name in ("kp", "vp", "ikp", "srp", "sip", "mkp", "mvp", "ks", "vs", "iks", "srs", "sis")}
    kv_pool = lambda c: c.transpose(0, 1, 3, 4, 2).reshape(c.shape[:2] + (ATTN_WIDTH, PAGE_SIZE))
    pool_kT = kv_pool(cache_k)
    pool_vT = kv_pool(cache_v)
    pool_ikT = cache_idx_k.transpose(0, 1, 3, 2)
    mem_k = cache_mem_k.reshape(depth, ns, n_mem * X_HEADS, X_HEAD_DIM)
    mem_v = cache_mem_v.reshape(depth, ns, n_mem * X_HEADS, X_HEAD_DIM)
    feature_major = lambda a, heads: a.reshape(nb, heads, -1, seq).transpose(0, 3, 1, 2)

    for l in range(depth):
        ws = _split_w_in(w_in[l])
        ssm_params = _ssm_params(ssm_a_re[l], ssm_a_im[l], ssm_b_re[l], ssm_b_im[l], ssm_c_re[l], ssm_c_im[l],
                                 ssm_log_dt[l])
        dense = (bf(w_up_attn[l]), bf(w_up_ssm[l]), bf(w_out[l]), norm_cross[l], bf(w_xq[l]))
        mlp_w = (bf(w_xo[l]), norm_mlp[l], bf(w_mlp_up[l]), bf(w_mlp_down[l]), norm_final)
        glu = bf(w_glu[l])

        qT, iqw, iwT, kT, vT, ikT, kb, vTb, ikb, u, gates = _proj_prompt(yp.reshape(nb, seq, d), norm_mix[l], ws, tm,
                                                                         tq)
        attn = _dsa_prompt(qT, iqw, iwT, ikb, kb, vTb, bias_prompt, tq)
        ssm, s_re, s_im = _ssm(u, zero_state, zero_state, ssm_params, ssm_d[l], glu, ts)
        yp, qc = _merge(yp, attn.reshape(nb * seq, -1), ssm.reshape(nb * seq, -1), gates.reshape(nb * seq, -1),
                        *dense, tm)
        mk, mv = _memkv(mem_prompt, norm_mem[l], bf(w_xk[l]), bf(w_xv[l]))
        oc = _cross(qc.reshape(nb, seq, -1), mk[None], mv[None], 0, _row_tile(seq, CROSS_Q_TILE))
        yp = _mlp(yp, oc.reshape(nb * seq, -1), *mlp_w, tm, l == depth - 1)
        outs["kp"].append(feature_major(kT, N_HEADS))
        outs["vp"].append(feature_major(vT, N_HEADS))
        outs["ikp"].append(ikT.transpose(0, 2, 1))
        outs["srp"].append(s_re.reshape(nb, groups, state))
        outs["sip"].append(s_im.reshape(nb, groups, state))
        outs["mkp"].append(mk.reshape(nb, n_mem, X_HEADS, X_HEAD_DIM))
        outs["mvp"].append(mv.reshape(nb, n_mem, X_HEADS, X_HEAD_DIM))

        q, iq, iw, k, v, ik, u, gates = _proj_rows(ys, norm_mix[l], ws)
        r3 = lambda a: a.reshape(ns, t_new, -1)
        attn = _dsa_sample(r3(q), r3(iq), r3(iw), r3(k), r3(v), r3(ik), pool_kT, pool_vT, pool_ikT, l,
                           page_table, bias_last, bias_new)
        ssm, s_re, s_im = _ssm(r3(u), state_ssm_re[l].reshape(ns, -1), state_ssm_im[l].reshape(ns, -1),
                               ssm_params, ssm_d[l], glu, t_new)
        ys, qc = _merge(ys, attn.reshape(ns * t_new, -1), ssm.reshape(ns * t_new, -1), gates, *dense, ns * t_new)
        oc = _cross(r3(qc), mem_k, mem_v, l, t_new)
        ys = _mlp(ys, oc.reshape(ns * t_new, -1), *mlp_w, ns * t_new, l == depth - 1)
        outs["ks"].append(k.reshape(ns, t_new, N_HEADS, HEAD_DIM))
        outs["vs"].append(v.reshape(ns, t_new, N_HEADS, HEAD_DIM))
        outs["iks"].append(ik.reshape(ns, t_new, IDX_DIM))
        outs["srs"].append(s_re.reshape(ns, groups, state))
        outs["sis"].append(s_im.reshape(ns, groups, state))

    st = lambda name: jnp.stack(outs[name])
    return (yp.reshape(nb, seq, d), ys.reshape(ns, t_new, d),
            st("kp"), st("vp"), st("ikp"), st("srp"), st("sip"), st("mkp"), st("mvp"),
            st("ks"), st("vs"), st("iks"), st("srs"), st("sis"))
```

```python
import functools
import math

import jax
import jax.numpy as jnp
import numpy as np
from jax import lax
from jax.experimental import pallas as pl
from jax.experimental.pallas import tpu as pltpu

F32 = jnp.float32
BF16 = jnp.bfloat16
I32 = jnp.int32

EPS = 1e-6
N_HEADS = 8
HEAD_DIM = 64
ATTN_WIDTH = N_HEADS * HEAD_DIM
IDX_HEADS = 8
IDX_DIM = 64
TOP_K_MAX = 256
SSM_GROUP = 16
SSM_STATE = 64
X_HEADS = 4
X_HEAD_DIM = 128
N_BUCKETS = 32
MAX_DISTANCE = 128
PAGE_SIZE = 128
V_ROWS = HEAD_DIM + 16

MXU_DEPTH = 256
LANES = 128
SUBLANES = 8
VMEM_LIMIT_BYTES = 56 * 1024 * 1024
INT_MIN = -(2 ** 31)
INT_MAX = 2 ** 31 - 1
EARLY_EXIT_BIT = 26
F32_MIN = float(np.finfo(np.float32).min)
F32_MAX = float(np.finfo(np.float32).max)
M_INIT = -1e30

NT_DIMS = (((1,), (1,)), ((), ()))


def _params(*semantics):
    return pltpu.CompilerParams(dimension_semantics=semantics, vmem_limit_bytes=VMEM_LIMIT_BYTES)


def _rmsnorm(x, g):
    ms = jnp.mean(x * x, axis=-1, keepdims=True)
    return x * lax.rsqrt(ms + EPS) * g


def _dot(a, b):
    return jnp.dot(a, b, preferred_element_type=F32)


def _dot_nt(a, b):
    return lax.dot_general(a, b, NT_DIMS, preferred_element_type=F32)


def _monotone_key(x):
    b = lax.bitcast_convert_type(x, I32)
    key = jnp.where(b < 0, b ^ INT_MAX, b)
    return jnp.where(x == 0.0, 0, key)


def _row_tile(rows, want):
    t = min(rows, want)
    assert rows % t == 0, (rows, t)
    return t


def _t5_bucket(dist):
    max_exact = N_BUCKETS // 2
    d = jnp.maximum(dist, 0)
    df = jnp.maximum(d, 1).astype(F32)
    large = max_exact + (jnp.log(df / max_exact) / math.log(MAX_DISTANCE / max_exact)
                         * (N_BUCKETS - max_exact)).astype(I32)
    large = jnp.minimum(large, N_BUCKETS - 1)
    return jnp.where(d < max_exact, d, large)


def _bias_lookup(rel_ref, bucket, h):
    out = jnp.zeros(bucket.shape, F32)
    for b in range(N_BUCKETS):
        out = jnp.where(bucket == b, rel_ref[b, h], out)
    return out - rel_ref[N_BUCKETS - 1, h]


def _bias_prompt_kernel(rel_ref, o_ref, *, tq):
    r = lax.broadcasted_iota(I32, (2 * tq, tq), 0)
    j = lax.broadcasted_iota(I32, (2 * tq, tq), 1)
    bucket = _t5_bucket(j - r + tq)
    for h in range(N_HEADS):
        o_ref[h] = _bias_lookup(rel_ref, bucket, h)


def _bias_sample_kernel(rel_ref, last_ref, new_ref, *, t_new):
    row = lax.broadcasted_iota(I32, (N_HEADS * t_new, PAGE_SIZE), 0)
    lane = lax.broadcasted_iota(I32, (N_HEADS * t_new, PAGE_SIZE), 1)
    last = jnp.zeros(row.shape, F32)
    new = jnp.zeros(row.shape, F32)
    for h in range(N_HEADS):
        t = row - h * t_new
        in_head = (t >= 0) & (t < t_new)
        last = jnp.where(in_head, _bias_lookup(rel_ref, _t5_bucket(PAGE_SIZE + t - lane), h), last)
        new = jnp.where(in_head, _bias_lookup(rel_ref, _t5_bucket(t - lane), h), new)
    last_ref[...] = last
    new_ref[...] = new


def _bias_tables(rel_bias, tq, t_new):
    smem = pl.BlockSpec(memory_space=pltpu.SMEM)
    prompt = pl.pallas_call(
        functools.partial(_bias_prompt_kernel, tq=tq),
        out_shape=jax.ShapeDtypeStruct((N_HEADS, 2 * tq, tq), F32),
        in_specs=[smem], name="bias_prompt")(rel_bias)
    last, new = pl.pallas_call(
        functools.partial(_bias_sample_kernel, t_new=t_new),
        out_shape=[jax.ShapeDtypeStruct((N_HEADS * t_new, PAGE_SIZE), F32)] * 2,
        in_specs=[smem], name="bias_sample")(rel_bias)
    return prompt, last, new


def _proj_prompt_kernel(x_ref, g_ref, wqT_ref, wiqT_ref, wiwT_ref, wk_ref, wkT_ref, wvT_ref, wik_ref, wikT_ref,
                        wu_ref, wg_ref,
                        qT_ref, iqw_ref, iwT_ref, kT_ref, vT_ref, ikT_ref, kb_ref, vTb_ref, ikb_ref, u_ref, gate_ref,
                        *, tq):
    h = _rmsnorm(x_ref[0], g_ref[...]).astype(BF16)
    qT_ref[0] = _dot_nt(wqT_ref[...], h).astype(BF16)
    iqT = _dot_nt(wiqT_ref[...], h).astype(BF16)
    for blk in range(iqw_ref.shape[1]):
        for hh in range(IDX_HEADS):
            iqw_ref[0, blk, :, hh * tq:(hh + 1) * tq] = iqT[hh * IDX_DIM:(hh + 1) * IDX_DIM, blk * tq:(blk + 1) * tq]
    iwT_ref[0] = _dot_nt(wiwT_ref[...], h) * (IDX_HEADS ** -0.5)
    kT_ref[0] = _dot_nt(wkT_ref[...], h)
    kb_ref[0] = _dot(h, wk_ref[...]).astype(BF16)
    vT = _dot_nt(wvT_ref[...], h)
    vT_ref[0] = vT
    ones = jnp.ones((V_ROWS - HEAD_DIM, vT.shape[1]), BF16)
    for hh in range(N_HEADS):
        vTb_ref[0, hh * V_ROWS:hh * V_ROWS + HEAD_DIM, :] = vT[hh * HEAD_DIM:(hh + 1) * HEAD_DIM, :].astype(BF16)
        vTb_ref[0, hh * V_ROWS + HEAD_DIM:(hh + 1) * V_ROWS, :] = ones
    ikT_ref[0] = _dot_nt(wikT_ref[...], h)
    ikb_ref[0] = _dot(h, wik_ref[...]).astype(BF16)
    u_ref[0] = _dot(h, wu_ref[...])
    gate_ref[0] = jax.nn.sigmoid(_dot(h, wg_ref[...]))


def _proj_rows_kernel(x_ref, g_ref, wq_ref, wiq_ref, wiw_ref, wk_ref, wv_ref, wik_ref, wu_ref, wg_ref,
                      q_ref, iq_ref, iw_ref, k_ref, v_ref, ik_ref, u_ref, gate_ref):
    h = _rmsnorm(x_ref[...], g_ref[...]).astype(BF16)
    q_ref[...] = _dot(h, wq_ref[...])
    iq_ref[...] = _dot(h, wiq_ref[...])
    iw_ref[...] = _dot(h, wiw_ref[...]) * (IDX_HEADS ** -0.5)
    k_ref[...] = _dot(h, wk_ref[...])
    v_ref[...] = _dot(h, wv_ref[...])
    ik_ref[...] = _dot(h, wik_ref[...])
    u_ref[...] = _dot(h, wu_ref[...])
    gate_ref[...] = jax.nn.sigmoid(_dot(h, wg_ref[...]))


def _split_w_in(w_in):
    sizes = (ATTN_WIDTH, ATTN_WIDTH, ATTN_WIDTH, IDX_HEADS * IDX_DIM, IDX_DIM, IDX_HEADS)
    offs = np.cumsum((0,) + sizes)
    d_model = w_in.shape[0]
    ssm_width = (w_in.shape[1] - offs[-1] - 2 * d_model)
    wq, wk, wv, wiq, wik, wiw = (w_in[:, offs[i]:offs[i + 1]] for i in range(6))
    wu = w_in[:, offs[-1]:offs[-1] + ssm_width]
    wg = w_in[:, offs[-1] + ssm_width:]
    wq = wq * (HEAD_DIM ** -0.5)
    wiq = wiq * (IDX_DIM ** -0.5)
    return tuple(w.astype(BF16) for w in (wq, wk, wv, wiq, wik, wiw, wu, wg))


def _full(shape):
    return pl.BlockSpec(shape, lambda *_: (0,) * len(shape), pipeline_mode=pl.Buffered(1))


def _proj_prompt(x, g, ws, tm, tq):
    n, t, d = x.shape
    wq, wk, wv, wiq, wik, wiw, wu, wg = ws
    sw, gw = wu.shape[1], wg.shape[1]
    assert tm % tq == 0
    weights = (wq.T, wiq.T, wiw.T, wk, wk.T, wv.T, wik, wik.T, wu, wg)
    rows = lambda w: pl.BlockSpec((1, tm, w), lambda b, i: (b, i, 0))
    cols = lambda w: pl.BlockSpec((1, w, tm), lambda b, i: (b, 0, i))
    iq_wide = IDX_HEADS * tq
    out_shape = [
        jax.ShapeDtypeStruct((n, ATTN_WIDTH, t), BF16),
        jax.ShapeDtypeStruct((n, t // tq, IDX_DIM, iq_wide), BF16),
        jax.ShapeDtypeStruct((n, IDX_HEADS, t), F32),
        jax.ShapeDtypeStruct((n, ATTN_WIDTH, t), F32),
        jax.ShapeDtypeStruct((n, ATTN_WIDTH, t), F32),
        jax.ShapeDtypeStruct((n, IDX_DIM, t), F32),
        jax.ShapeDtypeStruct((n, t, ATTN_WIDTH), BF16),
        jax.ShapeDtypeStruct((n, N_HEADS * V_ROWS, t), BF16),
        jax.ShapeDtypeStruct((n, t, IDX_DIM), BF16),
        jax.ShapeDtypeStruct((n, t, sw), F32),
        jax.ShapeDtypeStruct((n, t, gw), F32),
    ]
    out_specs = [cols(ATTN_WIDTH), pl.BlockSpec((1, tm // tq, IDX_DIM, iq_wide), lambda b, i: (b, i, 0, 0)),
                 cols(IDX_HEADS), cols(ATTN_WIDTH), cols(ATTN_WIDTH), cols(IDX_DIM), rows(ATTN_WIDTH),
                 cols(N_HEADS * V_ROWS), rows(IDX_DIM), rows(sw), rows(gw)]
    return pl.pallas_call(
        functools.partial(_proj_prompt_kernel, tq=tq), grid=(n, t // tm), out_shape=out_shape,
        in_specs=[rows(d), _full((1, d))] + [_full(w.shape) for w in weights],
        out_specs=out_specs, compiler_params=_params("parallel", "parallel"), name="proj_prompt",
    )(x, g.reshape(1, d), *weights)


def _proj_rows(x, g, ws):
    r, d = x.shape
    wq, wk, wv, wiq, wik, wiw, wu, wg = ws
    weights = (wq, wiq, wiw, wk, wv, wik, wu, wg)
    out_shape = [jax.ShapeDtypeStruct((r, w.shape[1]), F32) for w in weights]
    return pl.pallas_call(
        _proj_rows_kernel, out_shape=out_shape, compiler_params=_params(), name="proj_rows",
    )(x, g.reshape(1, d), *weights)


def _dsa_prompt_kernel(qT_ref, iqw_ref, iwT_ref, ikb_ref, kb_ref, vT_ref, bias_ref, o_ref,
                       s_ref, t_ref, pen_ref, pnear_ref, q2_ref, sbuf_ref, pbuf_ref, acc_ref, m_ref, l_ref,
                       *, tq, top_k):
    i = pl.program_id(1)
    gk = KEY_GROUP * tq
    n_grp = lax.div(i + KEY_GROUP, KEY_GROUP)
    n_far_grp = lax.div(jnp.maximum(i - 1, 0) + KEY_GROUP - 1, KEY_GROUP)
    iw = iwT_ref[0]
    row = lax.broadcasted_iota(I32, (tq, tq), 0)
    col = lax.broadcasted_iota(I32, (tq, tq), 1)
    grow = lax.broadcasted_iota(I32, (gk, tq), 0)

    ahead = row - col

    def score_group(g, has_future):
        for cc in range(KEY_GROUP):
            c = g * KEY_GROUP + cc
            r0 = pl.multiple_of(c * tq, tq)
            d = _dot(ikb_ref[0, pl.ds(r0, tq), :], iqw_ref[0, 0])
            sc = jnp.zeros((tq, tq), F32)
            for h in range(IDX_HEADS):
                sc = sc + iw[h:h + 1, :] * jnp.maximum(d[:, h * tq:(h + 1) * tq], 0.0)
            key = _monotone_key(sc)
            if has_future:
                key = jnp.where(ahead > (i - c) * tq, INT_MIN, key)
            s_ref[pl.ds(r0, tq), :] = key

    def past_group(g, carry):
        score_group(g, False)
        return carry

    lax.fori_loop(0, n_grp - 1, past_group, 0)
    score_group(n_grp - 1, True)

    def count(pred):
        def body(g, acc8):
            r0 = pl.multiple_of(g * gk, gk)
            ind = jnp.where(pred(s_ref[pl.ds(r0, gk), :], r0 + grow), 1, 0)
            return acc8 + ind.reshape(gk // SUBLANES, SUBLANES, tq).sum(axis=0)
        acc8 = lax.fori_loop(0, n_grp, body, jnp.zeros((SUBLANES, tq), I32))
        return acc8.sum(axis=0, keepdims=True)

    def bit_step(it, state):
        tu, cnt_ge = state
        cand_u = tu | lax.shift_left(jnp.int32(1), 31 - it)
        cand = cand_u ^ INT_MIN
        cnt = count(lambda blk, r0: blk >= cand)
        ok = cnt >= top_k
        return jnp.where(ok, cand_u, tu), jnp.where(ok, cnt, cnt_ge)

    state = lax.fori_loop(0, EARLY_EXIT_BIT, bit_step, (jnp.zeros((1, tq), I32), jnp.zeros((1, tq), I32)))
    t_ref[0:1, :], t_ref[1:2, :] = state
    unsettled = (state[1] != top_k) & (state[1] != 0)

    @pl.when(jnp.max(unsettled.astype(I32)) > 0)
    def _():
        t_ref[0:1, :], t_ref[1:2, :] = lax.fori_loop(EARLY_EXIT_BIT, 32, bit_step, state)

    tu, cnt_ge = t_ref[0:1, :], t_ref[1:2, :]
    thr = jnp.maximum(tu ^ INT_MIN, INT_MIN + 1)
    has_ties = jnp.max((cnt_ge > top_k).astype(I32)) > 0

    w_chunk = jnp.maximum(i - 1, 0)
    w0 = pl.multiple_of(w_chunk * tq, tq)

    @pl.when(jnp.logical_not(has_ties))
    def _():
        def pen_group(g, carry):
            r0 = pl.multiple_of(g * gk, gk)
            sel = (s_ref[pl.ds(r0, gk), :] >= thr) & (r0 + grow < (i - 1) * tq)
            pen_ref[pl.ds(r0, gk), :] = jnp.where(sel, 0.0, F32_MIN)
            return carry

        lax.fori_loop(0, jnp.maximum(n_far_grp, 1), pen_group, 0)
        pnear_ref[...] = jnp.where(s_ref[pl.ds(w0, 2 * tq), :] >= thr, 0.0, F32_MIN)

    @pl.when(has_ties)
    def _():
        need = (top_k - count(lambda blk, r0: blk > thr)).astype(F32)
        lower = jnp.where(row >= col, 1.0, 0.0).astype(BF16)
        def select_chunk(r0, seen):
            blk = s_ref[pl.ds(r0, tq), :]
            tie = blk == thr
            tie01 = jnp.where(tie, 1.0, 0.0)
            rank = _dot(lower, tie01.astype(BF16)) + seen
            seen = seen + tie01.reshape(tq // SUBLANES, SUBLANES, tq).sum(axis=0).sum(axis=0, keepdims=True)
            return (blk > thr) | (tie & (rank <= need)), seen

        def pen_group(g, carry):
            seen, seen_window = carry
            for cc in range(KEY_GROUP):
                c = g * KEY_GROUP + cc
                r0 = pl.multiple_of(c * tq, tq)
                seen_window = jnp.where(c == w_chunk, seen, seen_window)
                sel, seen = select_chunk(r0, seen)
                pen_ref[pl.ds(r0, tq), :] = jnp.where(sel & (c < i - 1), 0.0, F32_MIN)
            return seen, seen_window

        zeros = jnp.zeros((1, tq), F32)
        _, seen = lax.fori_loop(0, n_grp, pen_group, (zeros, zeros))
        for k in range(2):
            sel, seen = select_chunk(w0 + k * tq, seen)
            pnear_ref[k * tq:(k + 1) * tq, :] = jnp.where(sel, 0.0, F32_MIN)

    zero = jnp.zeros((HEAD_DIM, tq), BF16)
    for hp in range(N_HEADS // 2):
        lo = qT_ref[0, (2 * hp) * HEAD_DIM:(2 * hp + 1) * HEAD_DIM, :]
        hi = qT_ref[0, (2 * hp + 1) * HEAD_DIM:(2 * hp + 2) * HEAD_DIM, :]
        q2_ref[hp * 2 * HEAD_DIM:(hp + 1) * 2 * HEAD_DIM, :] = jnp.concatenate(
            [jnp.concatenate([lo, zero], axis=1), jnp.concatenate([zero, hi], axis=1)], axis=0)
    m_ref[...] = jnp.full(m_ref.shape, M_INIT, F32)
    l_ref[...] = jnp.zeros(l_ref.shape, F32)
    acc_ref[...] = jnp.zeros(acc_ref.shape, F32)

    def update_head(h, m8, r0, n_c):
        hs = slice(h * HEAD_DIM, (h + 1) * HEAD_DIM)
        ls = slice(h * SUBLANES, (h + 1) * SUBLANES)
        m_old = m_ref[h:h + 1, :]
        m_new = jnp.maximum(m_old, m8.max(axis=0, keepdims=True))
        alpha = jnp.exp(m_old - m_new)
        for cc in range(n_c):
            p = jnp.exp(sbuf_ref[h, cc * tq:(cc + 1) * tq, :] - m_new)
            pbuf_ref[h, cc * tq:(cc + 1) * tq, :] = p.astype(BF16)
        pv = _dot(vT_ref[0, h * V_ROWS:(h + 1) * V_ROWS, pl.ds(r0, n_c * tq)], pbuf_ref[h, 0:n_c * tq, :])
        acc_ref[hs, :] = alpha * acc_ref[hs, :] + pv[0:HEAD_DIM, :]
        l_ref[ls, :] = alpha * l_ref[ls, :] + pv[HEAD_DIM:HEAD_DIM + SUBLANES, :]
        m_ref[h:h + 1, :] = m_new

    def logits(hp, r0, n_c, pen_src, pen_r0, bias_rows):
        m8 = [None, None]
        for cc in range(n_c):
            pen = pen_src[pl.ds(pen_r0 + cc * tq, tq), :]
            kslab = kb_ref[0, pl.ds(r0 + cc * tq, tq), hp * 2 * HEAD_DIM:(hp + 1) * 2 * HEAD_DIM]
            s2 = _dot(kslab, q2_ref[hp * 2 * HEAD_DIM:(hp + 1) * 2 * HEAD_DIM, :])
            for e in range(2):
                h = 2 * hp + e
                s = s2[:, e * tq:(e + 1) * tq] + pen
                if bias_rows is not None:
                    s = s + bias_ref[h, pl.ds(bias_rows[cc], tq), :]
                sbuf_ref[h, cc * tq:(cc + 1) * tq, :] = s
                cm = s.reshape(tq // SUBLANES, SUBLANES, tq).max(axis=0)
                m8[e] = cm if m8[e] is None else jnp.maximum(m8[e], cm)
        return tuple(m8)

    def attend_group(m8, r0, n_c, pen_src, pen_r0, bias_rows, next_group_logits):
        pairs = N_HEADS // 2
        for hp in range(pairs):
            if hp + 1 < pairs:
                m8_next = logits(hp + 1, r0, n_c, pen_src, pen_r0, bias_rows)
            else:
                m8_next = next_group_logits() if next_group_logits is not None else None
            for e in range(2):
                update_head(2 * hp + e, m8[e], r0, n_c)
            m8 = m8_next
        return m8

    def far_logits0(g):
        r0 = pl.multiple_of(g * gk, gk)
        return logits(0, r0, KEY_GROUP, pen_ref, r0, None)

    def far_group(g, m8):
        r0 = pl.multiple_of(g * gk, gk)
        nxt = jnp.minimum(g + 1, n_far_grp - 1)
        return attend_group(m8, r0, KEY_GROUP, pen_ref, r0, None, lambda: far_logits0(nxt))

    def far_pair(gp, m8):
        g = 2 * gp
        r0 = pl.multiple_of(g * gk, gk)
        m8 = attend_group(m8, r0, KEY_GROUP, pen_ref, r0, None, lambda: far_logits0(g + 1))
        return far_group(g + 1, m8)

    n_pair = lax.div(n_far_grp, 2)
    m8 = lax.fori_loop(0, n_pair, far_pair, far_logits0(0))
    lax.fori_loop(2 * n_pair, n_far_grp, far_group, m8)
    first_bias = pl.multiple_of(jnp.where(i == 0, tq, 0), tq)
    near = (w0, 2, pnear_ref, 0, (first_bias, tq))
    attend_group(logits(0, *near), *near, None)

    for h in range(N_HEADS):
        hs = slice(h * HEAD_DIM, (h + 1) * HEAD_DIM)
        acc_ref[hs, :] = acc_ref[hs, :] / l_ref[h * SUBLANES:h * SUBLANES + 1, :]
    o_ref[0] = acc_ref[...].T.astype(BF16)


def _dsa_prompt(qT, iqw, iwT, ikb, kb, vT, bias, tq):
    n, _, t = qT.shape
    n_blk = t // tq
    assert t % tq == 0 and tq >= MAX_DISTANCE and n_blk % KEY_GROUP == 0 and n_blk >= 2, (t, tq)
    top_k = min(TOP_K_MAX, t // 4)
    gk = KEY_GROUP * tq
    colblk = lambda w: pl.BlockSpec((1, w, tq), lambda b, i: (b, 0, i))
    whole = lambda a: pl.BlockSpec((1,) + a.shape[1:], lambda b, i: (b, 0, 0), pipeline_mode=pl.Buffered(1))
    kern = functools.partial(_dsa_prompt_kernel, tq=tq, top_k=top_k)
    return pl.pallas_call(
        kern, grid=(n, n_blk),
        out_shape=jax.ShapeDtypeStruct((n, t, ATTN_WIDTH), BF16),
        in_specs=[colblk(ATTN_WIDTH), pl.BlockSpec((1, 1) + iqw.shape[2:], lambda b, i: (b, i, 0, 0)),
                  colblk(IDX_HEADS), whole(ikb), whole(kb), whole(vT),
                  pl.BlockSpec(bias.shape, lambda b, i: (0, 0, 0), pipeline_mode=pl.Buffered(1))],
        out_specs=pl.BlockSpec((1, tq, ATTN_WIDTH), lambda b, i: (b, i, 0)),
        scratch_shapes=[pltpu.VMEM((t, tq), I32),
                        pltpu.VMEM((SUBLANES, tq), I32),
                        pltpu.VMEM((t, tq), F32),
                        pltpu.VMEM((2 * tq, tq), F32),
                        pltpu.VMEM((N_HEADS * HEAD_DIM, 2 * tq), BF16),
                        pltpu.VMEM((N_HEADS, gk, tq), F32),
                        pltpu.VMEM((N_HEADS, gk, tq), BF16),
                        pltpu.VMEM((ATTN_WIDTH, tq), F32),
                        pltpu.VMEM((N_HEADS, tq), F32),
                        pltpu.VMEM((N_HEADS * SUBLANES, tq), F32)],
        compiler_params=_params("parallel", "arbitrary"), name="dsa_prompt",
    )(qT, iqw, iwT, ikb, kb, vT, bias)


def _page_specs(page_shape, layer, pages_per_step):
    def spec(r):
        return pl.BlockSpec((1, 1) + page_shape,
                            lambda b, g, pt: (layer, pt[b, g * pages_per_step + r]) + (0,) * len(page_shape))
    return [spec(r) for r in range(pages_per_step)]


def _head_sum(x, t_new):
    return x.reshape(IDX_HEADS, t_new, x.shape[-1]).sum(axis=0)


def _dsa_sample_score_kernel(pt_ref, *refs, pps, t_new):
    page_refs = refs[:pps]
    iq_ref, w_ref, iknew_ref, sp_ref, sn_ref = refs[pps:]
    g = pl.program_id(1)
    iq = iq_ref[0]
    w = w_ref[0]
    for r in range(pps):
        d = _dot(iq, page_refs[r][0, 0].astype(BF16))
        sc = _head_sum(w * jnp.maximum(d, 0.0), t_new)
        sp_ref[0, :, r * PAGE_SIZE:(r + 1) * PAGE_SIZE] = _monotone_key(sc)

    @pl.when(g == pl.num_programs(1) - 1)
    def _():
        d = _dot_nt(iq, iknew_ref[0].astype(BF16))
        sc = _head_sum(w * jnp.maximum(d, 0.0), t_new)
        t = lax.broadcasted_iota(I32, sc.shape, 0)
        j = lax.broadcasted_iota(I32, sc.shape, 1)
        sn_ref[0] = jnp.where(j > t, INT_MIN, _monotone_key(sc))


def _dsa_sample_thr_kernel(sp_ref, sn_ref, thr_ref, j_ref, *, top_k, idx_bits):
    rows, past = sp_ref.shape
    lane = lax.broadcasted_iota(I32, (rows, LANES), 1)

    def count(pred):
        tot = jnp.where(pred(sn_ref[...], past + lane), 1, 0)
        for c in range(past // LANES):
            tot = tot + jnp.where(pred(sp_ref[:, c * LANES:(c + 1) * LANES], c * LANES + lane), 1, 0)
        return jnp.broadcast_to(tot.sum(axis=-1, keepdims=True), (rows, LANES))

    def bit_step(it, tu):
        cand_u = tu | lax.shift_left(jnp.int32(1), 31 - it)
        cand = cand_u ^ INT_MIN
        cnt = count(lambda blk, idx: blk >= cand)
        return jnp.where(cnt >= top_k, cand_u, tu)

    tu = lax.fori_loop(0, 32, bit_step, jnp.zeros((rows, LANES), I32))
    thr = jnp.maximum(tu ^ INT_MIN, INT_MIN + 1)
    cnt_gt = count(lambda blk, idx: blk > thr)
    cnt_ge = count(lambda blk, idx: blk >= thr)
    need = top_k - cnt_gt
    multi = cnt_ge > top_k
    thr_ref[...] = thr
    j_ref[...] = jnp.full((rows, LANES), INT_MAX, I32)

    @pl.when(jnp.max(multi.astype(I32)) > 0)
    def _():
        def j_step(it, jv):
            cand = jv | lax.shift_left(jnp.int32(1), idx_bits - 1 - it)
            cnt = count(lambda blk, idx: (blk == thr) & (idx < cand))
            return jnp.where(cnt < need, cand, jv)
        jv = lax.fori_loop(0, idx_bits, j_step, jnp.zeros((rows, LANES), I32))
        j_ref[...] = jnp.where(multi, jv, INT_MAX)


def _dsa_sample_attn_kernel(pt_ref, *refs, pps, t_new):
    k_refs = refs[:pps]
    v_refs = refs[pps:2 * pps]
    (q_ref, sp_ref, sn_ref, thr_ref, j_ref, knew_ref, vnew_ref, blast_ref, bnew_ref,
     o_ref, acc_ref, m_ref, l_ref) = refs[2 * pps:]
    g = pl.program_id(1)
    last = g == pl.num_programs(1) - 1
    thr = thr_ref[0]
    jsel = j_ref[0]
    lane = lax.broadcasted_iota(I32, (t_new, PAGE_SIZE), 1)

    @pl.when(g == 0)
    def _():
        m_ref[...] = jnp.full(m_ref.shape, M_INIT, F32)
        l_ref[...] = jnp.zeros(l_ref.shape, F32)
        acc_ref[...] = jnp.zeros(acc_ref.shape, F32)

    lane_tiles = ATTN_WIDTH // LANES
    q = q_ref[0]

    def update(pages):
        logit = []
        for keys, idx, kT, _, bias in pages:
            s = _dot(q, kT.astype(BF16))
            if bias is not None:
                s = s + bias
            sel = (keys > thr) | ((keys == thr) & (idx <= jsel))
            cap = jnp.where(sel, F32_MAX, F32_MIN)
            logit.append(jnp.minimum(s, jnp.concatenate([cap] * N_HEADS, axis=0)))
        m_old = m_ref[...]
        m_new = jnp.maximum(m_old, jnp.broadcast_to(functools.reduce(jnp.maximum, logit).max(axis=-1, keepdims=True),
                                                    m_old.shape))
        prob = [jnp.exp(s - m_new) for s in logit]
        alpha = jnp.exp(m_old - m_new)
        l_ref[...] = alpha * l_ref[...] + jnp.broadcast_to(sum(prob).sum(axis=-1, keepdims=True), m_old.shape)
        pv = sum(_dot_nt(p.astype(BF16), page[3].astype(BF16)) for p, page in zip(prob, pages))
        acc_ref[...] = jnp.concatenate([alpha] * lane_tiles, axis=1) * acc_ref[...] + pv
        m_ref[...] = m_new

    last_f = jnp.where(last, 1.0, 0.0)
    update([(sp_ref[0, :, r * PAGE_SIZE:(r + 1) * PAGE_SIZE], (g * pps + r) * PAGE_SIZE + lane,
             k_refs[r][0, 0], v_refs[r][0, 0],
             blast_ref[...] * last_f if r == pps - 1 else None)
            for r in range(pps)])

    @pl.when(last)
    def _():
        n_past = pl.num_programs(1) * pps * PAGE_SIZE
        update([(sn_ref[0], n_past + lane, knew_ref[0], vnew_ref[0], bnew_ref[...])])
        o_full = acc_ref[...] / jnp.concatenate([l_ref[...]] * lane_tiles, axis=1)
        out_lane = lax.broadcasted_iota(I32, (t_new, ATTN_WIDTH), 1)
        out = jnp.zeros((t_new, ATTN_WIDTH), F32)
        for h in range(N_HEADS):
            own = (out_lane >= h * HEAD_DIM) & (out_lane < (h + 1) * HEAD_DIM)
            out = jnp.where(own, o_full[h * t_new:(h + 1) * t_new, :], out)
        o_ref[0] = out.astype(BF16)


def _dsa_sample(q, iq, iw, k_new, v_new, ik_new, pool_kT, pool_vT, pool_ikT, layer, page_table, bias_last,
                bias_new):
    n, t_new, _ = q.shape
    n_pages = page_table.shape[1]
    past = n_pages * PAGE_SIZE
    top_k = min(TOP_K_MAX, (past + t_new) // 4)
    ht = N_HEADS * t_new
    assert t_new <= PAGE_SIZE and IDX_HEADS == N_HEADS

    heads = lambda a, dim: a.reshape(n, t_new, N_HEADS, dim).transpose(0, 2, 1, 3)
    iq_hm = heads(iq, IDX_DIM).reshape(n, ht, IDX_DIM).astype(BF16)
    w_hm = jnp.broadcast_to(iw.transpose(0, 2, 1).reshape(n, ht, 1), (n, ht, LANES))
    eye = jnp.eye(N_HEADS, dtype=F32)
    q_bd = (heads(q, HEAD_DIM)[:, :, :, None, :] * eye[None, :, None, :, None]
            ).reshape(n, ht, ATTN_WIDTH).astype(BF16)
    page = lambda a: jnp.pad(heads(a, HEAD_DIM).transpose(0, 1, 3, 2).reshape(n, ATTN_WIDTH, t_new),
                             ((0, 0), (0, 0), (0, PAGE_SIZE - t_new)))
    k_page, v_page = page(k_new), page(v_new)
    ik_pad = jnp.pad(ik_new, ((0, 0), (0, PAGE_SIZE - t_new), (0, 0)))

    per_b = lambda a: pl.BlockSpec((1,) + a.shape[1:], lambda b, g, pt: (b,) + (0,) * (a.ndim - 1))
    const = lambda a: pl.BlockSpec(a.shape, lambda b, g, pt: (0,) * a.ndim)
    sp_spec = lambda pages: pl.BlockSpec((1, t_new, pages * PAGE_SIZE), lambda b, g, pt: (b, 0, g))

    pps = math.gcd(n_pages, SCORE_PAGES_PER_STEP)
    sp, sn = pl.pallas_call(
        functools.partial(_dsa_sample_score_kernel, pps=pps, t_new=t_new),
        grid_spec=pltpu.PrefetchScalarGridSpec(
            num_scalar_prefetch=1, grid=(n, n_pages // pps),
            in_specs=_page_specs((IDX_DIM, PAGE_SIZE), layer, pps) + [per_b(iq_hm), per_b(w_hm), per_b(ik_pad)],
            out_specs=[sp_spec(pps), pl.BlockSpec((1, t_new, PAGE_SIZE), lambda b, g, pt: (b, 0, 0))]),
        out_shape=[jax.ShapeDtypeStruct((n, t_new, past), I32), jax.ShapeDtypeStruct((n, t_new, PAGE_SIZE), I32)],
        compiler_params=_params("parallel", "arbitrary"), name="dsa_sample_score",
    )(page_table, *([pool_ikT] * pps), iq_hm, w_hm, ik_pad)

    rows = n * t_new
    rt = _row_tile(rows, THR_ROW_TILE)
    idx_bits = int(math.ceil(math.log2(past + PAGE_SIZE)))
    thr, jsel = pl.pallas_call(
        functools.partial(_dsa_sample_thr_kernel, top_k=top_k, idx_bits=idx_bits),
        grid=(rows // rt,),
        in_specs=[pl.BlockSpec((rt, past), lambda i: (i, 0)), pl.BlockSpec((rt, PAGE_SIZE), lambda i: (i, 0))],
        out_specs=[pl.BlockSpec((rt, LANES), lambda i: (i, 0))] * 2,
        out_shape=[jax.ShapeDtypeStruct((rows, LANES), I32)] * 2,
        compiler_params=_params("parallel"), name="dsa_sample_thr",
    )(sp.reshape(rows, past), sn.reshape(rows, PAGE_SIZE))
    thr = thr.reshape(n, t_new, LANES)
    jsel = jsel.reshape(n, t_new, LANES)

    kv_page = (ATTN_WIDTH, PAGE_SIZE)
    pps = math.gcd(n_pages, ATTN_PAGES_PER_STEP)
    return pl.pallas_call(
        functools.partial(_dsa_sample_attn_kernel, pps=pps, t_new=t_new),
        grid_spec=pltpu.PrefetchScalarGridSpec(
            num_scalar_prefetch=1, grid=(n, n_pages // pps),
            in_specs=(_page_specs(kv_page, layer, pps) + _page_specs(kv_page, layer, pps)
                      + [per_b(q_bd), sp_spec(pps), per_b(sn), per_b(thr), per_b(jsel), per_b(k_page), per_b(v_page),
                         const(bias_last), const(bias_new)]),
            out_specs=pl.BlockSpec((1, t_new, ATTN_WIDTH), lambda b, g, pt: (b, 0, 0)),
            scratch_shapes=[pltpu.VMEM((ht, ATTN_WIDTH), F32), pltpu.VMEM((ht, LANES), F32),
                            pltpu.VMEM((ht, LANES), F32)]),
        out_shape=jax.ShapeDtypeStruct((n, t_new, ATTN_WIDTH), BF16),
        compiler_params=_params("parallel", "arbitrary"), name="dsa_sample_attn",
    )(page_table, *([pool_kT] * pps), *([pool_vT] * pps), q_bd, sp, sn, thr, jsel, k_page, v_page, bias_last,
      bias_new)


def _gelu_tanh(y):
    return 0.5 * y * (1.0 + jnp.tanh(math.sqrt(2.0 / math.pi) * (y + 0.044715 * (y * y * y))))


def _ssm_kernel(u_ref, x0r_ref, x0i_ref, lr_ref, li_ref, b_ref, c_ref, d_ref, wglu_ref,
                y_ref, sr_ref, si_ref, x_ref, st_ref, *, tile):
    j = pl.program_id(1)
    ns = lr_ref.shape[1]
    seqs = range(u_ref.shape[0])

    @pl.when(j == 0)
    def _():
        for b in seqs:
            st_ref[b, 0:1, :] = x0r_ref[b]
            st_ref[b, 1:2, :] = x0i_ref[b]

    width = u_ref.shape[2]
    clusters = max(1, width // MXU_DEPTH)
    cw = width // clusters
    sw = ns // clusters
    for b in seqs:
        ub = u_ref[b].astype(BF16)
        for k in range(clusters):
            for part in range(2):
                cols = slice(part * ns + k * sw, part * ns + (k + 1) * sw)
                x_ref[b, :, cols] = _dot(ub[:, k * cw:(k + 1) * cw], b_ref[k * cw:(k + 1) * cw, cols])
    lr = lr_ref[...]
    li = li_ref[...]

    def step(t, carry):
        out = []
        for b in seqs:
            sr, si = carry[b]
            br = x_ref[b, pl.ds(t, 1), 0:ns]
            bi = x_ref[b, pl.ds(t, 1), ns:2 * ns]
            nr = lr * sr - li * si + br
            ni = lr * si + li * sr + bi
            x_ref[b, pl.ds(t, 1), 0:ns] = nr
            x_ref[b, pl.ds(t, 1), ns:2 * ns] = ni
            out.append((nr, ni))
        return tuple(out)

    final = lax.fori_loop(0, tile, step, tuple((st_ref[b, 0:1, :], st_ref[b, 1:2, :]) for b in seqs))
    for b in seqs:
        st_ref[b, 0:1, :] = final[b][0]
        st_ref[b, 1:2, :] = final[b][1]

    for b in seqs:
        cx = []
        for k in range(clusters):
            ch = slice(k * cw, (k + 1) * cw)
            re = slice(k * sw, (k + 1) * sw)
            im = slice(ns + k * sw, ns + (k + 1) * sw)
            cx.append(_dot(x_ref[b, :, re].astype(BF16), c_ref[re, ch])
                      + _dot(x_ref[b, :, im].astype(BF16), c_ref[im, ch]))
        y = jnp.concatenate(cx, axis=1) + d_ref[...] * u_ref[b]
        y = _gelu_tanh(y)
        y_ref[b] = (y * jax.nn.sigmoid(_dot(y.astype(BF16), wglu_ref[...]))).astype(BF16)

    @pl.when(j == pl.num_programs(1) - 1)
    def _():
        for b in seqs:
            sr_ref[b] = final[b][0]
            si_ref[b] = final[b][1]


def _ssm_params(a_re, a_im, b_re, b_im, c_re, c_im, log_dt):
    g = a_re.shape[0]
    dt = jnp.exp(log_dt)[:, None]
    mag = jnp.exp(a_re * dt)
    lam_re, lam_im = mag * jnp.cos(a_im * dt), mag * jnp.sin(a_im * dt)
    den = a_re * a_re + a_im * a_im
    nr, ni = lam_re - 1.0, lam_im
    f_re = (nr * a_re + ni * a_im) / den
    f_im = (ni * a_re - nr * a_im) / den
    bb_re = f_re[..., None] * b_re - f_im[..., None] * b_im
    bb_im = f_re[..., None] * b_im + f_im[..., None] * b_re
    eye = jnp.eye(g, dtype=F32)
    p, c = bb_re.shape[1], bb_re.shape[2]
    blk_in = lambda m: jnp.einsum('gpc,gh->gchp', m, eye).reshape(g * c, g * p)
    blk_out = lambda m: jnp.einsum('gcp,gh->gphc', m, eye).reshape(g * p, g * c)
    b_blk = jnp.concatenate([blk_in(bb_re), blk_in(bb_im)], axis=1).astype(BF16)
    c_blk = jnp.concatenate([blk_out(c_re), -blk_out(c_im)], axis=0).astype(BF16)
    return lam_re.reshape(1, g * p), lam_im.reshape(1, g * p), b_blk, c_blk


def _ssm(u, x0_re, x0_im, params, d_skip, w_glu, tile):
    n, t, w = u.shape
    lam_re, lam_im, b_blk, c_blk = params
    ns = lam_re.shape[1]
    x0_re = x0_re.reshape(n, 1, ns)
    x0_im = x0_im.reshape(n, 1, ns)
    nseq = math.gcd(n, SSM_SEQS_PER_STEP)
    seq = lambda width: pl.BlockSpec((nseq, tile, width), lambda b, j: (b, j, 0))
    state = pl.BlockSpec((nseq, 1, ns), lambda b, j: (b, 0, 0))
    y, sr, si = pl.pallas_call(
        functools.partial(_ssm_kernel, tile=tile), grid=(n // nseq, t // tile),
        in_specs=[seq(w), state, state, _full((1, ns)), _full((1, ns)), _full(b_blk.shape), _full(c_blk.shape),
                  _full((1, w)), _full(w_glu.shape)],
        out_specs=[seq(w), state, state],
        out_shape=[jax.ShapeDtypeStruct((n, t, w), BF16), jax.ShapeDtypeStruct((n, 1, ns), F32),
                   jax.ShapeDtypeStruct((n, 1, ns), F32)],
        scratch_shapes=[pltpu.VMEM((nseq, tile, 2 * ns), F32), pltpu.VMEM((nseq, SUBLANES, ns), F32)],
        compiler_params=_params("parallel", "arbitrary"), name="ssm",
    )(u, x0_re, x0_im, lam_re, lam_im, b_blk, c_blk, d_skip.reshape(1, w), w_glu)
    return y, sr, si


def _merge_kernel(x_ref, attn_ref, ssm_ref, gate_ref, wua_ref, wus_ref, wo_ref, gx_ref, wxq_ref, xo_ref, qc_ref):
    d = x_ref.shape[-1]
    gate = gate_ref[...]
    mixed = gate[:, 0:d] * _dot(attn_ref[...], wua_ref[...]) + gate[:, d:2 * d] * _dot(ssm_ref[...], wus_ref[...])
    x = x_ref[...] + _dot(mixed.astype(BF16), wo_ref[...])
    xo_ref[...] = x
    qc_ref[...] = _dot(_rmsnorm(x, gx_ref[...]).astype(BF16), wxq_ref[...]).astype(BF16)


def _merge(x, attn, ssm, gates, w_up_attn, w_up_ssm, w_out, g_cross, w_xq, tm):
    r, d = x.shape
    rows = lambda w: pl.BlockSpec((tm, w), lambda i: (i, 0))
    weights = (w_up_attn, w_up_ssm, w_out, g_cross.reshape(1, d), w_xq)
    return pl.pallas_call(
        _merge_kernel, grid=(r // tm,),
        in_specs=[rows(d), rows(attn.shape[1]), rows(ssm.shape[1]), rows(gates.shape[1])]
        + [_full(w.shape) for w in weights],
        out_specs=[rows(d), rows(w_xq.shape[1])],
        out_shape=[jax.ShapeDtypeStruct((r, d), F32), jax.ShapeDtypeStruct((r, w_xq.shape[1]), BF16)],
        compiler_params=_params("parallel"), name="merge",
    )(x, attn, ssm, gates, *weights)


def _cross_kernel(q_ref, mk_ref, mv_ref, o_ref):
    q = q_ref[0]
    outs = []
    for h in range(X_HEADS):
        sl = slice(h * X_HEAD_DIM, (h + 1) * X_HEAD_DIM)
        head_rows = pl.ds(h, mk_ref.shape[2] // X_HEADS, stride=X_HEADS)
        s = _dot_nt(q[:, sl], mk_ref[0, 0, head_rows, :].astype(BF16)) * (X_HEAD_DIM ** -0.5)
        s = s - s.max(axis=-1, keepdims=True)
        p = jnp.exp(s)
        p = (p / p.sum(axis=-1, keepdims=True)).astype(BF16)
        outs.append(_dot(p, mv_ref[0, 0, head_rows, :].astype(BF16)))
    o_ref[0] = jnp.concatenate(outs, axis=-1).astype(BF16)


def _cross(q, mk, mv, layer, tq):
    n, t, w = q.shape
    mem = pl.BlockSpec((1, 1) + mk.shape[2:], lambda b, i: (layer, b, 0, 0))
    blk = pl.BlockSpec((1, tq, w), lambda b, i: (b, i, 0))
    return pl.pallas_call(
        _cross_kernel, grid=(n, t // tq), in_specs=[blk, mem, mem], out_specs=blk,
        out_shape=jax.ShapeDtypeStruct((n, t, w), BF16),
        compiler_params=_params("parallel", "parallel"), name="cross",
    )(q, mk, mv)


def _mlp_kernel(x_ref, oc_ref, wxo_ref, gm_ref, wup_ref, wdn_ref, gf_ref, o_ref, *, ff_chunk, final):
    x = x_ref[...] + _dot(oc_ref[...], wxo_ref[...])
    h = _rmsnorm(x, gm_ref[...]).astype(BF16)
    acc = jnp.zeros(x.shape, F32)
    for c in range(wup_ref.shape[1] // ff_chunk):
        sl = slice(c * ff_chunk, (c + 1) * ff_chunk)
        a = jnp.maximum(_dot(h, wup_ref[:, sl]), 0.0)
        acc = acc + _dot((a * a).astype(BF16), wdn_ref[sl, :])
    x = x + acc
    o_ref[...] = _rmsnorm(x, gf_ref[...]) if final else x


def _mlp(x, o_cross, w_xo, g_mlp, w_up, w_down, g_final, tm, final):
    r, d = x.shape
    rows = lambda w: pl.BlockSpec((tm, w), lambda i: (i, 0))
    weights = (w_xo, g_mlp.reshape(1, d), w_up, w_down, g_final.reshape(1, d))
    return pl.pallas_call(
        functools.partial(_mlp_kernel, ff_chunk=min(1024, w_up.shape[1]), final=final), grid=(r // tm,),
        in_specs=[rows(d), rows(o_cross.shape[1])] + [_full(w.shape) for w in weights],
        out_specs=rows(d), out_shape=jax.ShapeDtypeStruct((r, d), F32),
        compiler_params=_params("parallel"), name="mlp",
    )(x, o_cross, *weights)


def _memkv_kernel(mem_ref, g_ref, wk_ref, wv_ref, mk_ref, mv_ref):
    hm = _rmsnorm(mem_ref[0], g_ref[...]).astype(BF16)
    mk = _dot(hm, wk_ref[...])
    mv = _dot(hm, wv_ref[...])
    for h in range(X_HEADS):
        head_rows = pl.ds(h, hm.shape[0], stride=X_HEADS)
        mk_ref[0, head_rows, :] = mk[:, h * X_HEAD_DIM:(h + 1) * X_HEAD_DIM]
        mv_ref[0, head_rows, :] = mv[:, h * X_HEAD_DIM:(h + 1) * X_HEAD_DIM]


def _memkv(mem, g, w_k, w_v):
    n, m, d = mem.shape
    w = w_k.shape[1]
    assert w == X_HEADS * X_HEAD_DIM
    out_blk = pl.BlockSpec((1, m * X_HEADS, X_HEAD_DIM), lambda b: (b, 0, 0))
    return pl.pallas_call(
        _memkv_kernel, grid=(n,),
        in_specs=[pl.BlockSpec((1, m, d), lambda b: (b, 0, 0)), _full((1, d)), _full(w_k.shape), _full(w_v.shape)],
        out_specs=[out_blk, out_blk], out_shape=[jax.ShapeDtypeStruct((n, m * X_HEADS, X_HEAD_DIM), F32)] * 2,
        compiler_params=_params("parallel"), name="memkv",
    )(mem, g.reshape(1, d), w_k, w_v)


PROMPT_ROW_TILE = 512
PROMPT_Q_TILE = 128
KEY_GROUP = 4
SSM_TIME_TILE = 256
SSM_SEQS_PER_STEP = 8
CROSS_Q_TILE = 512
THR_ROW_TILE = 128
SCORE_PAGES_PER_STEP = 32
ATTN_PAGES_PER_STEP = 16


def kernel(x_prompt, x_sample, mem_prompt, cache_k, cache_v, cache_idx_k, state_ssm_re, state_ssm_im,
           cache_mem_k, cache_mem_v, page_table, rel_bias, norm_mix, w_in, ssm_a_re, ssm_a_im,
           ssm_b_re, ssm_b_im, ssm_c_re, ssm_c_im, ssm_d, ssm_log_dt, w_glu, w_up_attn, w_up_ssm, w_out,
           norm_cross, norm_mem, w_xq, w_xk, w_xv, w_xo, norm_mlp, w_mlp_up, w_mlp_down, norm_final):
    depth = w_in.shape[0]
    nb, seq, d = x_prompt.shape
    ns, t_new, _ = x_sample.shape
    groups, state = ssm_a_re.shape[1], ssm_a_re.shape[2]
    n_mem = mem_prompt.shape[1]
    tm = _row_tile(seq, PROMPT_ROW_TILE)
    tq = _row_tile(seq, PROMPT_Q_TILE)
    ts = _row_tile(seq, SSM_TIME_TILE)
    bf = lambda a: a.astype(BF16)

    bias_prompt, bias_last, bias_new = _bias_tables(rel_bias, tq, t_new)
    yp = x_prompt.reshape(nb * seq, d)
    ys = x_sample.reshape(ns * t_new, d)
    zero_state = jnp.zeros((nb, groups * state), F32)
    outs = {name: [] for name in ("kp", "vp", "ikp", "srp", "sip", "mkp", "mvp", "ks", "vs", "iks", "srs", "sis")}
    kv_pool = lambda c: c.transpose(0, 1, 3, 4, 2).reshape(c.shape[:2] + (ATTN_WIDTH, PAGE_SIZE))
    pool_kT = kv_pool(cache_k)
    pool_vT = kv_pool(cache_v)
    pool_ikT = cache_idx_k.transpose(0, 1, 3, 2)
    mem_k = cache_mem_k.reshape(depth, ns, n_mem * X_HEADS, X_HEAD_DIM)
    mem_v = cache_mem_v.reshape(depth, ns, n_mem * X_HEADS, X_HEAD_DIM)
    feature_major = lambda a, heads: a.reshape(nb, heads, -1, seq).transpose(0, 3, 1, 2)

    for l in range(depth):
        ws = _split_w_in(w_in[l])
        ssm_params = _ssm_params(ssm_a_re[l], ssm_a_im[l], ssm_b_re[l], ssm_b_im[l], ssm_c_re[l], ssm_c_im[l],
                                 ssm_log_dt[l])
        dense = (bf(w_up_attn[l]), bf(w_up_ssm[l]), bf(w_out[l]), norm_cross[l], bf(w_xq[l]))
        mlp_w = (bf(w_xo[l]), norm_mlp[l], bf(w_mlp_up[l]), bf(w_mlp_down[l]), norm_final)
        glu = bf(w_glu[l])

        qT, iqw, iwT, kT, vT, ikT, kb, vTb, ikb, u, gates = _proj_prompt(yp.reshape(nb, seq, d), norm_mix[l], ws, tm,
                                                                         tq)
        attn = _dsa_prompt(qT, iqw, iwT, ikb, kb, vTb, bias_prompt, tq)
        ssm, s_re, s_im = _ssm(u, zero_state, zero_state, ssm_params, ssm_d[l], glu, ts)
        yp, qc = _merge(yp, attn.reshape(nb * seq, -1), ssm.reshape(nb * seq, -1), gates.reshape(nb * seq, -1),
                        *dense, tm)
        mk, mv = _memkv(mem_prompt, norm_mem[l], bf(w_xk[l]), bf(w_xv[l]))
        oc = _cross(qc.reshape(nb, seq, -1), mk[None], mv[None], 0, _row_tile(seq, CROSS_Q_TILE))
        yp = _mlp(yp, oc.reshape(nb * seq, -1), *mlp_w, tm, l == depth - 1)
        outs["kp"].append(feature_major(kT, N_HEADS))
        outs["vp"].append(feature_major(vT, N_HEADS))
        outs["ikp"].append(ikT.transpose(0, 2, 1))
        outs["srp"].append(s_re.reshape(nb, groups, state))
        outs["sip"].append(s_im.reshape(nb, groups, state))
        outs["mkp"].append(mk.reshape(nb, n_mem, X_HEADS, X_HEAD_DIM))
        outs["mvp"].append(mv.reshape(nb, n_mem, X_HEADS, X_HEAD_DIM))

        q, iq, iw, k, v, ik, u, gates = _proj_rows(ys, norm_mix[l], ws)
        r3 = lambda a: a.reshape(ns, t_new, -1)
        attn = _dsa_sample(r3(q), r3(iq), r3(iw), r3(k), r3(v), r3(ik), pool_kT, pool_vT, pool_ikT, l,
                           page_table, bias_last, bias_new)
        ssm, s_re, s_im = _ssm(r3(u), state_ssm_re[l].reshape(ns, -1), state_ssm_im[l].reshape(ns, -1),
                               ssm_params, ssm_d[l], glu, t_new)
        ys, qc = _merge(ys, attn.reshape(ns * t_new, -1), ssm.reshape(ns * t_new, -1), gates, *dense, ns * t_new)
        oc = _cross(r3(qc), mem_k, mem_v, l, t_new)
        ys = _mlp(ys, oc.reshape(ns * t_new, -1), *mlp_w, ns * t_new, l == depth - 1)
        outs["ks"].append(k.reshape(ns, t_new, N_HEADS, HEAD_DIM))
        outs["vs"].append(v.reshape(ns, t_new, N_HEADS, HEAD_DIM))
        outs["iks"].append(ik.reshape(ns, t_new, IDX_DIM))
        outs["srs"].append(s_re.reshape(ns, groups, state))
        outs["sis"].append(s_im.reshape(ns, groups, state))

    st = lambda name: jnp.stack(outs[name])
    return (yp.reshape(nb, seq, d), ys.reshape(ns, t_new, d),
            st("kp"), st("vp"), st("ikp"), st("srp"), st("sip"), st("mkp"), st("mvp"),
            st("ks"), st("vs"), st("iks"), st("srs"), st("sis"))
```
